```python
import jax, jax.numpy as jnp
from jax import lax
import numpy as np

D_MODEL = 4096
BATCH = 4
SEQ = 4096
DEPTH = 1
DEC_BATCH = 2
DEC_SEQ = 4096
PAST_LEN = 128

GRID_W = 64
HEAD_DIM = 128
N_Q_HEADS = 16
N_KV_HEADS = 4
ATTN_WIDTH = N_Q_HEADS * HEAD_DIM
KV_WIDTH = N_KV_HEADS * HEAD_DIM
CONV_WIDTH = D_MODEL - ATTN_WIDTH
CONV_GROUPS = 16
CONV_GROUP_DIM = CONV_WIDTH // CONV_GROUPS
CONV_K = 3
D_FF = 4 * D_MODEL
Q_BLOCK = 128
ROPE_THETA = 10000.0
ROPE_AXIS_DIM = HEAD_DIM // 2
EPS = 1e-6
IN_WIDTH = ATTN_WIDTH + 2 * KV_WIDTH + 3 * CONV_WIDTH

kernel_name = "hybrid_parallel_conv_axial_gqa_encoder"


def rmsnorm(x, g):
    xf = x.astype(jnp.float32)
    xf = xf * lax.rsqrt(jnp.mean(xf * xf, axis=-1, keepdims=True) + EPS)
    return (xf * g.astype(jnp.float32)).astype(x.dtype)


def axial_rope_tables(seq_len):
    n_rows = seq_len // GRID_W
    row = jnp.repeat(jnp.arange(n_rows, dtype=jnp.float32), GRID_W)
    col = jnp.tile(jnp.arange(GRID_W, dtype=jnp.float32), n_rows)
    freqs = ROPE_THETA ** (-jnp.arange(0, ROPE_AXIS_DIM, 2, dtype=jnp.float32) / ROPE_AXIS_DIM)
    ang = jnp.stack([row[:, None] * freqs, col[:, None] * freqs], axis=1)
    return jnp.cos(ang), jnp.sin(ang)


def apply_axial_rope(x, cos, sin):
    b, s, h, d = x.shape
    xr = x.astype(jnp.float32).reshape(b, s, h, 2, 2, ROPE_AXIS_DIM // 2)
    x1 = xr[..., 0, :]
    x2 = xr[..., 1, :]
    c = cos[None, :, None]
    sn = sin[None, :, None]
    out = jnp.stack([x1 * c - x2 * sn, x2 * c + x1 * sn], axis=-2)
    return out.reshape(b, s, h, d).astype(x.dtype)


def blocked_gqa(q, k, v):
    b, s, hq, d = q.shape
    g = hq // N_KV_HEADS
    n_blk = s // Q_BLOCK
    scale = HEAD_DIM ** -0.5
    qb = q.reshape(b, n_blk, Q_BLOCK, N_KV_HEADS, g, d).transpose(1, 0, 2, 3, 4, 5)

    def one_block(qi):
        sc = jnp.einsum('bqkgd,bskd->bkgqs', qi, k, preferred_element_type=jnp.float32) * scale
        p = jax.nn.softmax(sc, axis=-1).astype(v.dtype)
        return jnp.einsum('bkgqs,bskd->bqkgd', p, v)

    out = lax.map(one_block, qb)
    return out.transpose(1, 0, 2, 3, 4, 5).reshape(b, s, hq * d)


def centred_depthwise_conv3(u, w):
    up = jnp.pad(u, ((0, 0), (1, 1), (0, 0)))
    return w[0] * up[:, :-2] + w[1] * up[:, 1:-1] + w[2] * up[:, 2:]


def hybrid_layer(x, cos, sin, w_in, q_norm, k_norm, conv_w, attn_grp_norm,
                 conv_grp_norm, w_out, mix_norm, mlp_norm, w_up, w_down):
    b, s, _ = x.shape
    h = rmsnorm(x, mix_norm)
    z = h @ w_in
    cuts = np.cumsum([ATTN_WIDTH, KV_WIDTH, KV_WIDTH, CONV_WIDTH, CONV_WIDTH]).tolist()
    q, k, v, bg, cg, u = jnp.split(z, cuts, axis=-1)

    q = rmsnorm(q.reshape(b, s, N_Q_HEADS, HEAD_DIM), q_norm)
    k = rmsnorm(k.reshape(b, s, N_KV_HEADS, HEAD_DIM), k_norm)
    v = v.reshape(b, s, N_KV_HEADS, HEAD_DIM)
    q = apply_axial_rope(q, cos, sin)
    k = apply_axial_rope(k, cos, sin)
    a = blocked_gqa(q, k, v)
    a = rmsnorm(a.reshape(b, s, N_Q_HEADS, HEAD_DIM),
                attn_grp_norm.reshape(N_Q_HEADS, HEAD_DIM)).reshape(b, s, ATTN_WIDTH)

    c = bg * centred_depthwise_conv3(cg * u, conv_w)
    c = rmsnorm(c.reshape(b, s, CONV_GROUPS, CONV_GROUP_DIM),
                conv_grp_norm.reshape(CONV_GROUPS, CONV_GROUP_DIM)).reshape(b, s, CONV_WIDTH)

    x = x + jnp.concatenate([a, c], axis=-1) @ w_out

    h2 = rmsnorm(x, mlp_norm)
    x = x + jnp.square(jax.nn.relu(h2 @ w_up)) @ w_down
    return x


def run_trunk(x, w_in, q_norm, k_norm, conv_w, attn_grp_norm, conv_grp_norm,
              w_out, mix_norm, mlp_norm, w_up, w_down, final_norm):
    cos, sin = axial_rope_tables(x.shape[1])
    for l in range(DEPTH):
        x = hybrid_layer(x, cos, sin, w_in[l], q_norm[l], k_norm[l], conv_w[l],
                         attn_grp_norm[l], conv_grp_norm[l], w_out[l], mix_norm[l],
                         mlp_norm[l], w_up[l], w_down[l])
    return rmsnorm(x, final_norm)


def setup_inputs(seed: int = 0) -> dict:
    key = jax.random.key(seed)
    ks = jax.random.split(key, 16)
    f32 = jnp.float32

    def gain(k, shape):
        return 1.0 + 0.02 * jax.random.normal(k, shape, f32)

    return {
        "x_prompt": jax.random.normal(ks[0], (BATCH, SEQ, D_MODEL), f32),
        "x_sample": jax.random.normal(ks[1], (DEC_BATCH, DEC_SEQ, D_MODEL), f32),
        "w_in": jax.random.normal(ks[2], (DEPTH, D_MODEL, IN_WIDTH), f32) * D_MODEL ** -0.5,
        "q_norm": gain(ks[3], (DEPTH, HEAD_DIM)),
        "k_norm": gain(ks[4], (DEPTH, HEAD_DIM)),
        "conv_w": jax.random.normal(ks[5], (DEPTH, CONV_K, CONV_WIDTH), f32) * CONV_K ** -0.5,
        "attn_grp_norm": gain(ks[6], (DEPTH, ATTN_WIDTH)),
        "conv_grp_norm": gain(ks[7], (DEPTH, CONV_WIDTH)),
        "w_out": jax.random.normal(ks[8], (DEPTH, D_MODEL, D_MODEL), f32) * D_MODEL ** -0.5,
        "mix_norm": gain(ks[9], (DEPTH, D_MODEL)),
        "mlp_norm": gain(ks[10], (DEPTH, D_MODEL)),
        "w_up": jax.random.normal(ks[11], (DEPTH, D_MODEL, D_FF), f32) * D_MODEL ** -0.5,
        "w_down": jax.random.normal(ks[12], (DEPTH, D_FF, D_MODEL), f32) * D_FF ** -0.5,
        "final_norm": gain(ks[13], (D_MODEL,)),
    }


def reference(x_prompt, x_sample, w_in, q_norm, k_norm, conv_w, attn_grp_norm,
              conv_grp_norm, w_out, mix_norm, mlp_norm, w_up, w_down, final_norm):
    y_prompt = run_trunk(x_prompt, w_in, q_norm, k_norm, conv_w, attn_grp_norm,
                         conv_grp_norm, w_out, mix_norm, mlp_norm, w_up, w_down, final_norm)
    y_sample = run_trunk(x_sample, w_in, q_norm, k_norm, conv_w, attn_grp_norm,
                         conv_grp_norm, w_out, mix_norm, mlp_norm, w_up, w_down, final_norm)
    return (y_prompt, y_sample)
```

```python
import functools

import jax
import jax.numpy as jnp
from jax import lax
from jax.experimental import pallas as pl
from jax.experimental.pallas import tpu as pltpu

EPS = 1e-6
GRID_W = 64
ROPE_THETA = 10000.0
N_KV_HEADS = 4
CONV_GROUPS = 16

V7X_VMEM_BYTES = 64 * 1024 * 1024
VMEM_LIMIT_BYTES = V7X_VMEM_BYTES - 6 * 1024 * 1024

F32 = jnp.float32
BF16 = jnp.bfloat16


def _largest_tile(dim, target, quantum):
    t = min(target, dim)
    t -= t % quantum
    while t > quantum and dim % t:
        t -= quantum
    assert t >= quantum and dim % t == 0, (dim, target, quantum)
    return t


def _rope_tables(seq_len, head_dim):
    axis_dim = head_dim // 2
    n_rows = seq_len // GRID_W
    row = jnp.repeat(jnp.arange(n_rows, dtype=F32), GRID_W)
    col = jnp.tile(jnp.arange(GRID_W, dtype=F32), n_rows)
    freqs = ROPE_THETA ** (-jnp.arange(0, axis_dim, 2, dtype=F32) / axis_dim)
    ang_r = row[:, None] * freqs
    ang_c = col[:, None] * freqs
    cos = jnp.concatenate([jnp.cos(ang_r), jnp.cos(ang_r), jnp.cos(ang_c), jnp.cos(ang_c)], axis=-1)
    sin = jnp.concatenate([-jnp.sin(ang_r), jnp.sin(ang_r), -jnp.sin(ang_c), jnp.sin(ang_c)], axis=-1)
    return cos, sin


def _rms_scale(x):
    return lax.rsqrt(jnp.mean(x * x, axis=-1, keepdims=True) + EPS)


def _in_proj_kernel(x_ref, g_ref, w_ref, qg_ref, kg_ref, cos_ref, sin_ref, z_ref, h_ref, *,
                    n_q_tiles, n_k_tiles, head_dim, q_scale):
    j = pl.program_id(1)

    @pl.when(j == 0)
    def _():
        xf = x_ref[...]
        h_ref[...] = (xf * _rms_scale(xf) * g_ref[...]).astype(BF16)

    acc = jnp.dot(h_ref[...], w_ref[...], preferred_element_type=F32)
    tn = acc.shape[1]
    quarter = head_dim // 4

    def norm_rope(gain, post_scale):
        cos = cos_ref[...]
        sin = sin_ref[...]
        lane = lax.broadcasted_iota(jnp.int32, cos.shape, 1)
        first_half = (lane & quarter) == 0
        for c in range(tn // head_dim):
            a = acc[:, c * head_dim:(c + 1) * head_dim]
            a = a * _rms_scale(a) * gain
            up = pltpu.roll(a, head_dim - quarter, axis=1)
            down = pltpu.roll(a, quarter, axis=1)
            a = a * cos + jnp.where(first_half, up, down) * sin
            if post_scale != 1.0:
                a = a * post_scale
            z_ref[:, c * head_dim:(c + 1) * head_dim] = a.astype(BF16)

    @pl.when(j < n_q_tiles)
    def _():
        norm_rope(qg_ref[...], q_scale)

    @pl.when((j >= n_q_tiles) & (j < n_q_tiles + n_k_tiles))
    def _():
        norm_rope(kg_ref[...], 1.0)

    @pl.when(j >= n_q_tiles + n_k_tiles)
    def _():
        z_ref[...] = acc.astype(BF16)


def _in_proj(x2, mix_g, w_in, q_g, k_g, cos, sin, *, seq_len, attn_width, kv_width):
    m, d = x2.shape
    n = w_in.shape[1]
    head_dim = q_g.shape[-1]
    tm = _largest_tile(seq_len, 512, 16)
    tn = _largest_tile(kv_width, 512, head_dim)
    assert attn_width % tn == 0 and n % tn == 0
    blocks_per_seq = seq_len // tm
    kern = functools.partial(
        _in_proj_kernel, n_q_tiles=attn_width // tn, n_k_tiles=kv_width // tn,
        head_dim=head_dim, q_scale=float(head_dim) ** -0.5)
    return pl.pallas_call(
        kern,
        grid=(m // tm, n // tn),
        in_specs=[
            pl.BlockSpec((tm, d), lambda i, j: (i, 0)),
            pl.BlockSpec((1, d), lambda i, j: (0, 0)),
            pl.BlockSpec((d, tn), lambda i, j: (0, j)),
            pl.BlockSpec((1, head_dim), lambda i, j: (0, 0)),
            pl.BlockSpec((1, head_dim), lambda i, j: (0, 0)),
            pl.BlockSpec((tm, head_dim), lambda i, j: (i % blocks_per_seq, 0)),
            pl.BlockSpec((tm, head_dim), lambda i, j: (i % blocks_per_seq, 0)),
        ],
        out_specs=pl.BlockSpec((tm, tn), lambda i, j: (i, j)),
        out_shape=jax.ShapeDtypeStruct((m, n), BF16),
        scratch_shapes=[pltpu.VMEM((tm, d), BF16)],
        compiler_params=pltpu.CompilerParams(
            dimension_semantics=("parallel", "arbitrary"), vmem_limit_bytes=VMEM_LIMIT_BYTES),
        name="in_proj",
    )(x2, mix_g, w_in, q_g, k_g, cos, sin)


def _attention_kernel(q_ref, k_ref, v_ref, g_ref, o_ref, *, group, head_dim):
    tq = q_ref.shape[0]
    q = q_ref[...]
    qs = jnp.concatenate([q[:, h * head_dim:(h + 1) * head_dim] for h in range(group)], axis=0)
    s = lax.dot_general(qs, k_ref[...], (((1,), (1,)), ((), ())), preferred_element_type=F32)
    m = jnp.max(s, axis=-1, keepdims=True)
    p = jnp.exp(s - m)
    l = jnp.sum(p, axis=-1, keepdims=True)
    o = jnp.dot(p.astype(BF16), v_ref[...], preferred_element_type=F32) / l
    o = o * _rms_scale(o)
    for h in range(group):
        gain = g_ref[:, h * head_dim:(h + 1) * head_dim]
        o_ref[:, h * head_dim:(h + 1) * head_dim] = (o[h * tq:(h + 1) * tq] * gain).astype(BF16)


def _attention(z, attn_g, *, n_batch, seq_len, attn_width, kv_width, head_dim):
    m = z.shape[0]
    n_kv = N_KV_HEADS
    group = attn_width // head_dim // n_kv
    assert kv_width == n_kv * head_dim
    tq = _largest_tile(seq_len, 128, 16)
    q_blocks = seq_len // tq
    gw = group * head_dim
    k_col0 = attn_width // head_dim
    v_col0 = (attn_width + kv_width) // head_dim
    kern = functools.partial(_attention_kernel, group=group, head_dim=head_dim)
    return pl.pallas_call(
        kern,
        grid=(n_batch, n_kv, q_blocks),
        in_specs=[
            pl.BlockSpec((tq, gw), lambda b, h, i: (b * q_blocks + i, h)),
            pl.BlockSpec((seq_len, head_dim), lambda b, h, i: (b, k_col0 + h)),
            pl.BlockSpec((seq_len, head_dim), lambda b, h, i: (b, v_col0 + h)),
            pl.BlockSpec((1, gw), lambda b, h, i: (0, h)),
        ],
        out_specs=pl.BlockSpec((tq, gw), lambda b, h, i: (b * q_blocks + i, h)),
        out_shape=jax.ShapeDtypeStruct((m, attn_width), BF16),
        compiler_params=pltpu.CompilerParams(
            dimension_semantics=("parallel", "parallel", "arbitrary"),
            vmem_limit_bytes=VMEM_LIMIT_BYTES),
        name="attention",
    )(z, z, z, attn_g)


def _conv_kernel(b_ref, c_ref, u_ref, w_ref, g_ref, o_ref):
    s = b_ref.shape[0]
    cu = c_ref[...].astype(F32) * u_ref[...].astype(F32)
    t = lax.broadcasted_iota(jnp.int32, cu.shape, 0)
    prev = jnp.where(t == 0, 0.0, pltpu.roll(cu, 1, axis=0))
    nxt = jnp.where(t == s - 1, 0.0, pltpu.roll(cu, s - 1, axis=0))
    w = w_ref[...]
    conv = w[0:1] * prev + w[1:2] * cu + w[2:3] * nxt
    c = b_ref[...].astype(F32) * conv
    o_ref[...] = (c * _rms_scale(c) * g_ref[...]).astype(BF16)


def _conv(z, conv_w, conv_g, *, n_batch, seq_len, attn_width, kv_width, conv_width, group_dim):
    m = z.shape[0]
    tc = group_dim
    col0 = (attn_width + 2 * kv_width) // tc
    n_c = conv_width // tc
    return pl.pallas_call(
        _conv_kernel,
        grid=(n_batch, n_c),
        in_specs=[
            pl.BlockSpec((seq_len, tc), lambda b, c: (b, col0 + c)),
            pl.BlockSpec((seq_len, tc), lambda b, c: (b, col0 + n_c + c)),
            pl.BlockSpec((seq_len, tc), lambda b, c: (b, col0 + 2 * n_c + c)),
            pl.BlockSpec((conv_w.shape[0], tc), lambda b, c: (0, c)),
            pl.BlockSpec((1, tc), lambda b, c: (0, c)),
        ],
        out_specs=pl.BlockSpec((seq_len, tc), lambda b, c: (b, c)),
        out_shape=jax.ShapeDtypeStruct((m, conv_width), BF16),
        compiler_params=pltpu.CompilerParams(
            dimension_semantics=("parallel", "parallel"), vmem_limit_bytes=VMEM_LIMIT_BYTES),
        name="gated_conv",
    )(z, z, z, conv_w, conv_g)


def _out_proj_kernel(x_ref, a_ref, c_ref, wa_ref, wc_ref, o_ref):
    acc = jnp.dot(a_ref[...], wa_ref[...], preferred_element_type=F32)
    acc += jnp.dot(c_ref[...], wc_ref[...], preferred_element_type=F32)
    o_ref[...] = x_ref[...] + acc


def _out_proj(x2, a, c, w_out):
    m, d = x2.shape
    ka = a.shape[1]
    kc = c.shape[1]
    assert ka == kc and w_out.shape[0] == ka + kc
    tm = _largest_tile(m, 1024, 16)
    tn = _largest_tile(d, 1024, 128)
    return pl.pallas_call(
        _out_proj_kernel,
        grid=(m // tm, d // tn),
        in_specs=[
            pl.BlockSpec((tm, tn), lambda i, j: (i, j)),
            pl.BlockSpec((tm, ka), lambda i, j: (i, 0)),
            pl.BlockSpec((tm, kc), lambda i, j: (i, 0)),
            pl.BlockSpec((ka, tn), lambda i, j: (0, j)),
            pl.BlockSpec((kc, tn), lambda i, j: (1, j)),
        ],
        out_specs=pl.BlockSpec((tm, tn), lambda i, j: (i, j)),
        out_shape=jax.ShapeDtypeStruct((m, d), F32),
        compiler_params=pltpu.CompilerParams(
            dimension_semantics=("parallel", "arbitrary"), vmem_limit_bytes=VMEM_LIMIT_BYTES),
        name="out_proj",
    )(x2, a, c, w_out, w_out)


def _mlp_kernel(x_ref, g_ref, wu_ref, wd_ref, fg_ref, o_ref, h_ref):
    f = pl.program_id(1)

    @pl.when(f == 0)
    def _():
        xf = x_ref[...]
        h_ref[...] = (xf * _rms_scale(xf) * g_ref[...]).astype(BF16)
        o_ref[...] = xf

    u = jnp.dot(h_ref[...], wu_ref[...], preferred_element_type=F32)
    act = jnp.square(jnp.maximum(u, 0.0)).astype(BF16)
    o_ref[...] += jnp.dot(act, wd_ref[...], preferred_element_type=F32)

    @pl.when(f == pl.num_programs(1) - 1)
    def _():
        y = o_ref[...]
        o_ref[...] = y * _rms_scale(y) * fg_ref[...]


def _mlp(x1, mlp_g, w_up, w_down, final_g):
    m, d = x1.shape
    d_ff = w_up.shape[1]
    tm = _largest_tile(m, 512, 16)
    tf = _largest_tile(d_ff, 512, 128)
    return pl.pallas_call(
        _mlp_kernel,
        grid=(m // tm, d_ff // tf),
        in_specs=[
            pl.BlockSpec((tm, d), lambda i, f: (i, 0), pipeline_mode=pl.Buffered(1)),
            pl.BlockSpec((1, d), lambda i, f: (0, 0)),
            pl.BlockSpec((d, tf), lambda i, f: (0, f)),
            pl.BlockSpec((tf, d), lambda i, f: (f, 0)),
            pl.BlockSpec((1, d), lambda i, f: (0, 0)),
        ],
        out_specs=pl.BlockSpec((tm, d), lambda i, f: (i, 0)),
        out_shape=jax.ShapeDtypeStruct((m, d), F32),
        scratch_shapes=[pltpu.VMEM((tm, d), BF16)],
        compiler_params=pltpu.CompilerParams(
            dimension_semantics=("parallel", "arbitrary"), vmem_limit_bytes=VMEM_LIMIT_BYTES),
        name="mlp",
    )(x1, mlp_g, w_up, w_down, final_g)


def _trunk(x, w_in, q_g, k_g, conv_w, attn_g, conv_g, w_out, mix_g, mlp_g, w_up, w_down, final_g, cos, sin):
    n_batch, seq_len, d = x.shape
    head_dim = q_g.shape[-1]
    attn_width = attn_g.shape[-1]
    conv_width = conv_g.shape[-1]
    kv_width = (w_in.shape[1] - attn_width - 3 * conv_width) // 2
    x2 = x.reshape(n_batch * seq_len, d)
    z = _in_proj(x2, mix_g, w_in, q_g, k_g, cos, sin,
                 seq_len=seq_len, attn_width=attn_width, kv_width=kv_width)
    a = _attention(z, attn_g, n_batch=n_batch, seq_len=seq_len, attn_width=attn_width,
                   kv_width=kv_width, head_dim=head_dim)
    c = _conv(z, conv_w, conv_g, n_batch=n_batch, seq_len=seq_len, attn_width=attn_width,
              kv_width=kv_width, conv_width=conv_width, group_dim=conv_width // CONV_GROUPS)
    x1 = _out_proj(x2, a, c, w_out)
    y = _mlp(x1, mlp_g, w_up, w_down, final_g)
    return y.reshape(n_batch, seq_len, d)


def kernel(x_prompt, x_sample, w_in, q_norm, k_norm, conv_w, attn_grp_norm, conv_grp_norm, w_out,
           mix_norm, mlp_norm, w_up, w_down, final_norm):
    assert w_in.shape[0] == 1, "single-layer trunk"
    head_dim = q_norm.shape[-1]
    params = (
        w_in[0].astype(BF16), q_norm, k_norm, conv_w[0], attn_grp_norm, conv_grp_norm,
        w_out[0].astype(BF16), mix_norm, mlp_norm, w_up[0].astype(BF16), w_down[0].astype(BF16),
        final_norm.reshape(1, -1),
    )
    outs = []
    for x in (x_prompt, x_sample):
        cos, sin = _rope_tables(x.shape[1], head_dim)
        outs.append(_trunk(x, *params, cos, sin))
    return tuple(outs)
```

```python
import functools

import jax
import jax.numpy as jnp
from jax import lax
from jax.experimental import pallas as pl
from jax.experimental.pallas import tpu as pltpu

EPS = 1e-6
GRID_W = 64
ROPE_THETA = 10000.0
N_KV_HEADS = 4
CONV_GROUPS = 16
LOG2_E = 1.4426950408889634

V7X_VMEM_BYTES = 64 * 1024 * 1024
VMEM_LIMIT_BYTES = V7X_VMEM_BYTES - 6 * 1024 * 1024

F32 = jnp.float32
BF16 = jnp.bfloat16


def _largest_tile(dim, target, quantum):
    t = min(target, dim)
    t -= t % quantum
    while t > quantum and dim % t:
        t -= quantum
    assert t >= quantum and dim % t == 0, (dim, target, quantum)
    return t


def _rope_tables(seq_len, head_dim):
    axis_dim = head_dim // 2
    n_rows = seq_len // GRID_W
    row = jnp.repeat(jnp.arange(n_rows, dtype=F32), GRID_W)
    col = jnp.tile(jnp.arange(GRID_W, dtype=F32), n_rows)
    freqs = ROPE_THETA ** (-jnp.arange(0, axis_dim, 2, dtype=F32) / axis_dim)
    ang_r = row[:, None] * freqs
    ang_c = col[:, None] * freqs
    cos = jnp.concatenate([jnp.cos(ang_r), jnp.cos(ang_r), jnp.cos(ang_c), jnp.cos(ang_c)], axis=-1)
    sin = jnp.concatenate([-jnp.sin(ang_r), jnp.sin(ang_r), -jnp.sin(ang_c), jnp.sin(ang_c)], axis=-1)
    return cos, sin


def _rms_scale(x):
    return lax.rsqrt(jnp.mean(x * x, axis=-1, keepdims=True) + EPS)


def _in_proj_kernel(x_ref, g_ref, w_ref, z_ref, h_ref):
    @pl.when(pl.program_id(1) == 0)
    def _():
        xf = x_ref[...]
        h_ref[...] = (xf * _rms_scale(xf) * g_ref[...]).astype(BF16)

    z_ref[...] = jnp.dot(h_ref[...], w_ref[...], preferred_element_type=F32).astype(BF16)


def _in_proj(x2, mix_g, w_in):
    m, d = x2.shape
    n = w_in.shape[1]
    tm = _largest_tile(m, 512, 16)
    tn = _largest_tile(n, 1024, 128)
    return pl.pallas_call(
        _in_proj_kernel,
        grid=(m // tm, n // tn),
        in_specs=[
            pl.BlockSpec((tm, d), lambda i, j: (i, 0)),
            pl.BlockSpec((1, d), lambda i, j: (0, 0)),
            pl.BlockSpec((d, tn), lambda i, j: (0, j)),
        ],
        out_specs=pl.BlockSpec((tm, tn), lambda i, j: (i, j)),
        out_shape=jax.ShapeDtypeStruct((m, n), BF16),
        scratch_shapes=[pltpu.VMEM((tm, d), BF16)],
        compiler_params=pltpu.CompilerParams(
            dimension_semantics=("parallel", "arbitrary"), vmem_limit_bytes=VMEM_LIMIT_BYTES),
        name="in_proj",
    )(x2, mix_g, w_in)


def _norm_rope(a, gain, cos, sin):
    quarter = a.shape[1] // 4
    a = a * _rms_scale(a) * gain
    lane = lax.broadcasted_iota(jnp.int32, a.shape, 1)
    up = pltpu.roll(a, a.shape[1] - quarter, axis=1)
    down = pltpu.roll(a, quarter, axis=1)
    return a * cos + jnp.where((lane & quarter) == 0, up, down) * sin


def _attention_kernel(q_ref, k_ref, v_ref, qg_ref, kg_ref, cos_ref, sin_ref, g_ref, o_ref, kt_ref, vx_ref, *,
                      group, head_dim, unit_rows):
    seq_len = k_ref.shape[0]
    tq = q_ref.shape[0]
    i = pl.program_id(2)

    @pl.when(i == 0)
    def _():
        k = _norm_rope(k_ref[...].astype(F32), kg_ref[...], cos_ref[...], sin_ref[...])
        kt_ref[...] = k.astype(BF16).T
        vx_ref[:, :head_dim] = v_ref[...]
        vx_ref[:, head_dim:] = jnp.ones((seq_len, head_dim), BF16)

    q_scale = float(head_dim) ** -0.5 * LOG2_E
    for h in range(group):
        cols = slice(h * head_dim, (h + 1) * head_dim)
        for r in range(0, tq, unit_rows):
            rows = slice(r, r + unit_rows)
            pos = pl.ds(pl.multiple_of(i * tq + r, unit_rows), unit_rows)
            q = _norm_rope(q_ref[rows, cols].astype(F32), qg_ref[...], cos_ref[pos, :], sin_ref[pos, :])
            s = jnp.dot((q * q_scale).astype(BF16), kt_ref[...], preferred_element_type=F32)
            m = jnp.max(s, axis=-1, keepdims=True)
            p = jnp.exp2(s - m).astype(BF16)
            ol = jnp.dot(p, vx_ref[...], preferred_element_type=F32)
            o = ol[:, :head_dim] / ol[:, head_dim:]
            o_ref[rows, cols] = (o * _rms_scale(o) * g_ref[:, cols]).astype(BF16)


def _attention(z, q_g, k_g, cos, sin, attn_g, *, n_batch, seq_len, attn_width, kv_width):
    m = z.shape[0]
    head_dim = q_g.shape[-1]
    n_kv = N_KV_HEADS
    group = attn_width // head_dim // n_kv
    assert kv_width == n_kv * head_dim
    tq = _largest_tile(seq_len, 512, 16)
    q_blocks = seq_len // tq
    gw = group * head_dim
    k_col0 = attn_width // head_dim
    v_col0 = (attn_width + kv_width) // head_dim
    kern = functools.partial(_attention_kernel, group=group, head_dim=head_dim,
                             unit_rows=_largest_tile(tq, 256, 16))
    whole = lambda b, h, i: (0, 0)
    return pl.pallas_call(
        kern,
        grid=(n_batch, n_kv, q_blocks),
        in_specs=[
            pl.BlockSpec((tq, gw), lambda b, h, i: (b * q_blocks + i, h)),
            pl.BlockSpec((seq_len, head_dim), lambda b, h, i: (b, k_col0 + h)),
            pl.BlockSpec((seq_len, head_dim), lambda b, h, i: (b, v_col0 + h)),
            pl.BlockSpec((1, head_dim), whole),
            pl.BlockSpec((1, head_dim), whole),
            pl.BlockSpec((seq_len, head_dim), whole),
            pl.BlockSpec((seq_len, head_dim), whole),
            pl.BlockSpec((1, gw), lambda b, h, i: (0, h)),
        ],
        out_specs=pl.BlockSpec((tq, gw), lambda b, h, i: (b * q_blocks + i, h)),
        out_shape=jax.ShapeDtypeStruct((m, attn_width), BF16),
        scratch_shapes=[pltpu.VMEM((head_dim, seq_len), BF16), pltpu.VMEM((seq_len, 2 * head_dim), BF16)],
        compiler_params=pltpu.CompilerParams(
            dimension_semantics=("parallel", "parallel", "arbitrary"),
            vmem_limit_bytes=VMEM_LIMIT_BYTES),
        name="attention",
    )(z, z, z, q_g, k_g, cos, sin, attn_g)


def _conv_kernel(b_ref, c_ref, u_ref, w_ref, g_ref, o_ref):
    s = b_ref.shape[0]
    cu = c_ref[...].astype(F32) * u_ref[...].astype(F32)
    t = lax.broadcasted_iota(jnp.int32, cu.shape, 0)
    prev = jnp.where(t == 0, 0.0, pltpu.roll(cu, 1, axis=0))
    nxt = jnp.where(t == s - 1, 0.0, pltpu.roll(cu, s - 1, axis=0))
    w = w_ref[...]
    conv = w[0:1] * prev + w[1:2] * cu + w[2:3] * nxt
    c = b_ref[...].astype(F32) * conv
    o_ref[...] = (c * _rms_scale(c) * g_ref[...]).astype(BF16)


def _conv(z, conv_w, conv_g, *, n_batch, seq_len, attn_width, kv_width, conv_width, group_dim):
    m = z.shape[0]
    tc = group_dim
    col0 = (attn_width + 2 * kv_width) // tc
    n_c = conv_width // tc
    return pl.pallas_call(
        _conv_kernel,
        grid=(n_batch, n_c),
        in_specs=[
            pl.BlockSpec((seq_len, tc), lambda b, c: (b, col0 + c)),
            pl.BlockSpec((seq_len, tc), lambda b, c: (b, col0 + n_c + c)),
            pl.BlockSpec((seq_len, tc), lambda b, c: (b, col0 + 2 * n_c + c)),
            pl.BlockSpec((conv_w.shape[0], tc), lambda b, c: (0, c)),
            pl.BlockSpec((1, tc), lambda b, c: (0, c)),
        ],
        out_specs=pl.BlockSpec((seq_len, tc), lambda b, c: (b, c)),
        out_shape=jax.ShapeDtypeStruct((m, conv_width), BF16),
        compiler_params=pltpu.CompilerParams(
            dimension_semantics=("parallel", "parallel"), vmem_limit_bytes=VMEM_LIMIT_BYTES),
        name="gated_conv",
    )(z, z, z, conv_w, conv_g)


def _out_proj_kernel(x_ref, a_ref, c_ref, wa_ref, wc_ref, o_ref):
    acc = jnp.dot(a_ref[...], wa_ref[...], preferred_element_type=F32)
    acc += jnp.dot(c_ref[...], wc_ref[...], preferred_element_type=F32)
    o_ref[...] = x_ref[...] + acc


def _out_proj(x2, a, c, w_out):
    m, d = x2.shape
    ka = a.shape[1]
    kc = c.shape[1]
    assert ka == kc and w_out.shape[0] == ka + kc
    tm = _largest_tile(m, 1024, 16)
    tn = _largest_tile(d, 1024, 128)
    return pl.pallas_call(
        _out_proj_kernel,
        grid=(m // tm, d // tn),
        in_specs=[
            pl.BlockSpec((tm, tn), lambda i, j: (i, j)),
            pl.BlockSpec((tm, ka), lambda i, j: (i, 0)),
            pl.BlockSpec((tm, kc), lambda i, j: (i, 0)),
            pl.BlockSpec((ka, tn), lambda i, j: (0, j)),
            pl.BlockSpec((kc, tn), lambda i, j: (1, j)),
        ],
        out_specs=pl.BlockSpec((tm, tn), lambda i, j: (i, j)),
        out_shape=jax.ShapeDtypeStruct((m, d), F32),
        compiler_params=pltpu.CompilerParams(
            dimension_semantics=("parallel", "arbitrary"), vmem_limit_bytes=VMEM_LIMIT_BYTES),
        name="out_proj",
    )(x2, a, c, w_out, w_out)


def _mlp_kernel(x_ref, g_ref, wu_ref, wd_ref, fg_ref, o_ref, h_ref):
    f = pl.program_id(1)

    @pl.when(f == 0)
    def _():
        xf = x_ref[...]
        h_ref[...] = (xf * _rms_scale(xf) * g_ref[...]).astype(BF16)
        o_ref[...] = xf

    u = jnp.dot(h_ref[...], wu_ref[...], preferred_element_type=F32)
    act = jnp.square(jnp.maximum(u, 0.0)).astype(BF16)
    o_ref[...] += jnp.dot(act, wd_ref[...], preferred_element_type=F32)

    @pl.when(f == pl.num_programs(1) - 1)
    def _():
        y = o_ref[...]
        o_ref[...] = y * _rms_scale(y) * fg_ref[...]


def _mlp(x1, mlp_g, w_up, w_down, final_g):
    m, d = x1.shape
    d_ff = w_up.shape[1]
    tm = _largest_tile(m, 512, 16)
    tf = _largest_tile(d_ff, 512, 128)
    return pl.pallas_call(
        _mlp_kernel,
        grid=(m // tm, d_ff // tf),
        in_specs=[
            pl.BlockSpec((tm, d), lambda i, f: (i, 0), pipeline_mode=pl.Buffered(1)),
            pl.BlockSpec((1, d), lambda i, f: (0, 0)),
            pl.BlockSpec((d, tf), lambda i, f: (0, f)),
            pl.BlockSpec((tf, d), lambda i, f: (f, 0)),
            pl.BlockSpec((1, d), lambda i, f: (0, 0)),
        ],
        out_specs=pl.BlockSpec((tm, d), lambda i, f: (i, 0)),
        out_shape=jax.ShapeDtypeStruct((m, d), F32),
        scratch_shapes=[pltpu.VMEM((tm, d), BF16)],
        compiler_params=pltpu.CompilerParams(
            dimension_semantics=("parallel", "arbitrary"), vmem_limit_bytes=VMEM_LIMIT_BYTES),
        name="mlp",
    )(x1, mlp_g, w_up, w_down, final_g)


def _trunk(x, w_in, q_g, k_g, conv_w, attn_g, conv_g, w_out, mix_g, mlp_g, w_up, w_down, final_g, cos, sin):
    n_batch, seq_len, d = x.shape
    attn_width = attn_g.shape[-1]
    conv_width = conv_g.shape[-1]
    kv_width = (w_in.shape[1] - attn_width - 3 * conv_width) // 2
    x2 = x.reshape(n_batch * seq_len, d)
    z = _in_proj(x2, mix_g, w_in)
    a = _attention(z, q_g, k_g, cos, sin, attn_g, n_batch=n_batch, seq_len=seq_len,
                   attn_width=attn_width, kv_width=kv_width)
    c = _conv(z, conv_w, conv_g, n_batch=n_batch, seq_len=seq_len, attn_width=attn_width,
              kv_width=kv_width, conv_width=conv_width, group_dim=conv_width // CONV_GROUPS)
    x1 = _out_proj(x2, a, c, w_out)
    y = _mlp(x1, mlp_g, w_up, w_down, final_g)
    return y.reshape(n_batch, seq_len, d)


def kernel(x_prompt, x_sample, w_in, q_norm, k_norm, conv_w, attn_grp_norm, conv_grp_norm, w_out,
           mix_norm, mlp_norm, w_up, w_down, final_norm):
    assert w_in.shape[0] == 1, "single-layer trunk"
    head_dim = q_norm.shape[-1]
    params = (
        w_in[0].astype(BF16), q_norm, k_norm, conv_w[0], attn_grp_norm, conv_grp_norm,
        w_out[0].astype(BF16), mix_norm, mlp_norm, w_up[0].astype(BF16), w_down[0].astype(BF16),
        final_norm.reshape(1, -1),
    )
    outs = []
    for x in (x_prompt, x_sample):
        cos, sin = _rope_tables(x.shape[1], head_dim)
        outs.append(_trunk(x, *params, cos, sin))
    return tuple(outs)
```

```python
import functools

import jax
import jax.numpy as jnp
from jax import lax
from jax.experimental import pallas as pl
from jax.experimental.pallas import tpu as pltpu

EPS = 1e-6
GRID_W = 64
ROPE_THETA = 10000.0
N_KV_HEADS = 4
CONV_GROUPS = 16
LOG2_E = 1.4426950408889634

V7X_VMEM_BYTES = 64 * 1024 * 1024
VMEM_LIMIT_BYTES = V7X_VMEM_BYTES - 6 * 1024 * 1024

F32 = jnp.float32
BF16 = jnp.bfloat16
BF16_SUBLANES = 16


def _largest_tile(dim, target, quantum):
    t = min(target, dim)
    t -= t % quantum
    while t > quantum and dim % t:
        t -= quantum
    assert t >= quantum and dim % t == 0, (dim, target, quantum)
    return t


def _rope_tables(seq_len, head_dim):
    axis_dim = head_dim // 2
    n_rows = seq_len // GRID_W
    row = jnp.repeat(jnp.arange(n_rows, dtype=F32), GRID_W)
    col = jnp.tile(jnp.arange(GRID_W, dtype=F32), n_rows)
    freqs = ROPE_THETA ** (-jnp.arange(0, axis_dim, 2, dtype=F32) / axis_dim)
    ang_r = row[:, None] * freqs
    ang_c = col[:, None] * freqs
    cos = jnp.concatenate([jnp.cos(ang_r), jnp.cos(ang_r), jnp.cos(ang_c), jnp.cos(ang_c)], axis=-1)
    sin = jnp.concatenate([-jnp.sin(ang_r), jnp.sin(ang_r), -jnp.sin(ang_c), jnp.sin(ang_c)], axis=-1)
    return cos, sin


def _rms_scale(x):
    return lax.rsqrt(jnp.mean(x * x, axis=-1, keepdims=True) + EPS)


def _in_proj_kernel(x_ref, g_ref, w_ref, z_ref, h_ref):
    @pl.when(pl.program_id(1) == 0)
    def _():
        xf = x_ref[...]
        h_ref[...] = (xf * _rms_scale(xf) * g_ref[...]).astype(BF16)

    z_ref[...] = jnp.dot(h_ref[...], w_ref[...], preferred_element_type=F32).astype(BF16)


def _in_proj(x2, mix_g, w_in):
    m, d = x2.shape
    n = w_in.shape[1]
    tm = _largest_tile(m, 512, 16)
    tn = _largest_tile(n, 1024, 128)
    return pl.pallas_call(
        _in_proj_kernel,
        grid=(m // tm, n // tn),
        in_specs=[
            pl.BlockSpec((tm, d), lambda i, j: (i, 0)),
            pl.BlockSpec((1, d), lambda i, j: (0, 0)),
            pl.BlockSpec((d, tn), lambda i, j: (0, j)),
        ],
        out_specs=pl.BlockSpec((tm, tn), lambda i, j: (i, j)),
        out_shape=jax.ShapeDtypeStruct((m, n), BF16),
        scratch_shapes=[pltpu.VMEM((tm, d), BF16)],
        compiler_params=pltpu.CompilerParams(
            dimension_semantics=("parallel", "arbitrary"), vmem_limit_bytes=VMEM_LIMIT_BYTES),
        name="in_proj",
    )(x2, mix_g, w_in)


def _norm_rope(a, gain, cos, sin):
    quarter = a.shape[1] // 4
    a = a * _rms_scale(a) * gain
    lane = lax.broadcasted_iota(jnp.int32, a.shape, 1)
    up = pltpu.roll(a, a.shape[1] - quarter, axis=1)
    down = pltpu.roll(a, quarter, axis=1)
    return a * cos + jnp.where((lane & quarter) == 0, up, down) * sin


def _attention_kernel(q_ref, k_ref, v_ref, qg_ref, kg_ref, cos_ref, sin_ref, g_ref, *refs,
                      group, head_dim, unit_rows, ramp_rows, n_cast):
    cast_in, (o_ref, *cast_out), (kt_ref, vx_ref) = refs[:n_cast], refs[n_cast:2 * n_cast + 1], refs[2 * n_cast + 1:]
    seq_len = k_ref.shape[0]
    tq = q_ref.shape[0]
    i = pl.program_id(2)

    @pl.when(i == 0)
    def _():
        k = _norm_rope(k_ref[...].astype(F32), kg_ref[...], cos_ref[...], sin_ref[...])
        kt_ref[...] = k.astype(BF16).T
        vx_ref[:, :head_dim] = v_ref[...]
        vx_ref[:, head_dim:] = jnp.ones((seq_len, head_dim), BF16)

    q_scale = float(head_dim) ** -0.5 * LOG2_E
    for h in range(group):
        cols = slice(h * head_dim, (h + 1) * head_dim)
        bounds = list(range(0, tq + 1, unit_rows))
        if h == 0 and unit_rows > ramp_rows:
            bounds.insert(1, ramp_rows)
        if h == group - 1 and unit_rows > ramp_rows:
            bounds.insert(-1, tq - ramp_rows)
        for r, r_end in zip(bounds[:-1], bounds[1:]):
            rows = slice(r, r_end)
            pos = pl.ds(pl.multiple_of(i * tq + r, ramp_rows), r_end - r)
            q = _norm_rope(q_ref[rows, cols].astype(F32), qg_ref[...], cos_ref[pos, :], sin_ref[pos, :])
            s = jnp.dot((q * q_scale).astype(BF16), kt_ref[...], preferred_element_type=F32)
            m = jnp.max(s, axis=-1, keepdims=True)
            p = jnp.exp2(s - m).astype(BF16)
            ol = jnp.dot(p, vx_ref[...], preferred_element_type=F32)
            o = ol[:, :head_dim] / ol[:, head_dim:]
            o_ref[rows, cols] = (o * _rms_scale(o) * g_ref[:, cols]).astype(BF16)

    for w_in_ref, w_out_ref in zip(cast_in, cast_out):
        w_out_ref[...] = w_in_ref[...].astype(BF16)


def _attention(z, q_g, k_g, cos, sin, attn_g, cast_weights, *, n_batch, seq_len, attn_width, kv_width):
    m = z.shape[0]
    head_dim = q_g.shape[-1]
    n_kv = N_KV_HEADS
    group = attn_width // head_dim // n_kv
    assert kv_width == n_kv * head_dim
    tq = _largest_tile(seq_len, 512, 16)
    q_blocks = seq_len // tq
    gw = group * head_dim
    k_col0 = attn_width // head_dim
    v_col0 = (attn_width + kv_width) // head_dim
    n_steps = n_batch * n_kv * q_blocks
    kern = functools.partial(_attention_kernel, group=group, head_dim=head_dim,
                             unit_rows=_largest_tile(tq, 256, 16), ramp_rows=_largest_tile(tq, 64, 16),
                             n_cast=len(cast_weights))
    whole = lambda b, h, i: (0, 0)
    q_map = lambda b, h, i: (b * q_blocks + i, h)
    step_map = lambda b, h, i: ((b * n_kv + h) * q_blocks + i, 0)
    cast_specs = []
    for w in cast_weights:
        assert w.shape[0] % (n_steps * BF16_SUBLANES) == 0, (w.shape, n_steps)
        cast_specs.append(pl.BlockSpec((w.shape[0] // n_steps, w.shape[1]), step_map))
    return pl.pallas_call(
        kern,
        grid=(n_batch, n_kv, q_blocks),
        in_specs=[
            pl.BlockSpec((tq, gw), q_map),
            pl.BlockSpec((seq_len, head_dim), lambda b, h, i: (b, k_col0 + h)),
            pl.BlockSpec((seq_len, head_dim), lambda b, h, i: (b, v_col0 + h)),
            pl.BlockSpec((1, head_dim), whole),
            pl.BlockSpec((1, head_dim), whole),
            pl.BlockSpec((seq_len, head_dim), whole),
            pl.BlockSpec((seq_len, head_dim), whole),
            pl.BlockSpec((1, gw), lambda b, h, i: (0, h)),
            *cast_specs,
        ],
        out_specs=[pl.BlockSpec((tq, gw), q_map), *cast_specs],
        out_shape=[jax.ShapeDtypeStruct((m, attn_width), BF16),
                   *(jax.ShapeDtypeStruct(w.shape, BF16) for w in cast_weights)],
        scratch_shapes=[pltpu.VMEM((head_dim, seq_len), BF16), pltpu.VMEM((seq_len, 2 * head_dim), BF16)],
        compiler_params=pltpu.CompilerParams(
            dimension_semantics=("arbitrary", "arbitrary", "arbitrary"),
            vmem_limit_bytes=VMEM_LIMIT_BYTES),
        name="attention",
    )(z, z, z, q_g, k_g, cos, sin, attn_g, *cast_weights)


def _conv_kernel(b_ref, c_ref, u_ref, w_ref, g_ref, o_ref):
    s = b_ref.shape[0]
    cu = c_ref[...].astype(F32) * u_ref[...].astype(F32)
    t = lax.broadcasted_iota(jnp.int32, cu.shape, 0)
    prev = jnp.where(t == 0, 0.0, pltpu.roll(cu, 1, axis=0))
    nxt = jnp.where(t == s - 1, 0.0, pltpu.roll(cu, s - 1, axis=0))
    w = w_ref[...]
    conv = w[0:1] * prev + w[1:2] * cu + w[2:3] * nxt
    c = b_ref[...].astype(F32) * conv
    o_ref[...] = (c * _rms_scale(c) * g_ref[...]).astype(BF16)


def _conv(z, conv_w, conv_g, *, n_batch, seq_len, attn_width, kv_width, conv_width, group_dim):
    m = z.shape[0]
    tc = group_dim
    col0 = (attn_width + 2 * kv_width) // tc
    n_c = conv_width // tc
    return pl.pallas_call(
        _conv_kernel,
        grid=(n_batch, n_c),
        in_specs=[
            pl.BlockSpec((seq_len, tc), lambda b, c: (b, col0 + c)),
            pl.BlockSpec((seq_len, tc), lambda b, c: (b, col0 + n_c + c)),
            pl.BlockSpec((seq_len, tc), lambda b, c: (b, col0 + 2 * n_c + c)),
            pl.BlockSpec((conv_w.shape[0], tc), lambda b, c: (0, c)),
            pl.BlockSpec((1, tc), lambda b, c: (0, c)),
        ],
        out_specs=pl.BlockSpec((seq_len, tc), lambda b, c: (b, c)),
        out_shape=jax.ShapeDtypeStruct((m, conv_width), BF16),
        compiler_params=pltpu.CompilerParams(
            dimension_semantics=("parallel", "parallel"), vmem_limit_bytes=VMEM_LIMIT_BYTES),
        name="gated_conv",
    )(z, z, z, conv_w, conv_g)


def _out_proj_kernel(x_ref, a_ref, c_ref, wa_ref, wc_ref, o_ref):
    acc = jnp.dot(a_ref[...], wa_ref[...], preferred_element_type=F32)
    acc += jnp.dot(c_ref[...], wc_ref[...], preferred_element_type=F32)
    o_ref[...] = x_ref[...] + acc


def _out_proj(x2, a, c, w_out):
    m, d = x2.shape
    ka = a.shape[1]
    kc = c.shape[1]
    assert ka == kc and w_out.shape[0] == ka + kc
    tm = _largest_tile(m, 1024, 16)
    tn = _largest_tile(d, 1024, 128)
    return pl.pallas_call(
        _out_proj_kernel,
        grid=(m // tm, d // tn),
        in_specs=[
            pl.BlockSpec((tm, tn), lambda i, j: (i, j)),
            pl.BlockSpec((tm, ka), lambda i, j: (i, 0)),
            pl.BlockSpec((tm, kc), lambda i, j: (i, 0)),
            pl.BlockSpec((ka, tn), lambda i, j: (0, j)),
            pl.BlockSpec((kc, tn), lambda i, j: (1, j)),
        ],
        out_specs=pl.BlockSpec((tm, tn), lambda i, j: (i, j)),
        out_shape=jax.ShapeDtypeStruct((m, d), F32),
        compiler_params=pltpu.CompilerParams(
            dimension_semantics=("parallel", "arbitrary"), vmem_limit_bytes=VMEM_LIMIT_BYTES),
        name="out_proj",
    )(x2, a, c, w_out, w_out)


def _mlp_kernel(x_ref, g_ref, wu_ref, wd_ref, fg_ref, o_ref, h_ref):
    f = pl.program_id(1)

    @pl.when(f == 0)
    def _():
        xf = x_ref[...]
        h_ref[...] = (xf * _rms_scale(xf) * g_ref[...]).astype(BF16)
        o_ref[...] = xf

    u = jnp.dot(h_ref[...], wu_ref[...], preferred_element_type=F32)
    act = jnp.square(jnp.maximum(u, 0.0)).astype(BF16)
    o_ref[...] += jnp.dot(act, wd_ref[...], preferred_element_type=F32)

    @pl.when(f == pl.num_programs(1) - 1)
    def _():
        y = o_ref[...]
        o_ref[...] = y * _rms_scale(y) * fg_ref[...]


def _mlp(x1, mlp_g, w_up, w_down, final_g):
    m, d = x1.shape
    d_ff = w_up.shape[1]
    tm = _largest_tile(m, 512, 16)
    tf = _largest_tile(d_ff, 512, 128)
    return pl.pallas_call(
        _mlp_kernel,
        grid=(m // tm, d_ff // tf),
        in_specs=[
            pl.BlockSpec((tm, d), lambda i, f: (i, 0), pipeline_mode=pl.Buffered(1)),
            pl.BlockSpec((1, d), lambda i, f: (0, 0)),
            pl.BlockSpec((d, tf), lambda i, f: (0, f)),
            pl.BlockSpec((tf, d), lambda i, f: (f, 0)),
            pl.BlockSpec((1, d), lambda i, f: (0, 0)),
        ],
        out_specs=pl.BlockSpec((tm, d), lambda i, f: (i, 0)),
        out_shape=jax.ShapeDtypeStruct((m, d), F32),
        scratch_shapes=[pltpu.VMEM((tm, d), BF16)],
        compiler_params=pltpu.CompilerParams(
            dimension_semantics=("parallel", "arbitrary"), vmem_limit_bytes=VMEM_LIMIT_BYTES),
        name="mlp",
    )(x1, mlp_g, w_up, w_down, final_g)


def _trunk(x, w_in, q_g, k_g, conv_w, attn_g, conv_g, mix_g, mlp_g, final_g, cos, sin, late_weights, cast_late):
    n_batch, seq_len, d = x.shape
    attn_width = attn_g.shape[-1]
    conv_width = conv_g.shape[-1]
    kv_width = (w_in.shape[1] - attn_width - 3 * conv_width) // 2
    x2 = x.reshape(n_batch * seq_len, d)
    z = _in_proj(x2, mix_g, w_in)
    a, *cast = _attention(z, q_g, k_g, cos, sin, attn_g, late_weights if cast_late else (),
                          n_batch=n_batch, seq_len=seq_len, attn_width=attn_width, kv_width=kv_width)
    w_out, w_up, w_down = cast if cast_late else late_weights
    c = _conv(z, conv_w, conv_g, n_batch=n_batch, seq_len=seq_len, attn_width=attn_width,
              kv_width=kv_width, conv_width=conv_width, group_dim=conv_width // CONV_GROUPS)
    x1 = _out_proj(x2, a, c, w_out)
    y = _mlp(x1, mlp_g, w_up, w_down, final_g)
    return y.reshape(n_batch, seq_len, d), (w_out, w_up, w_down)


def kernel(x_prompt, x_sample, w_in, q_norm, k_norm, conv_w, attn_grp_norm, conv_grp_norm, w_out,
           mix_norm, mlp_norm, w_up, w_down, final_norm):
    assert w_in.shape[0] == 1, "single-layer trunk"
    head_dim = q_norm.shape[-1]
    params = (w_in[0].astype(BF16), q_norm, k_norm, conv_w[0], attn_grp_norm, conv_grp_norm,
              mix_norm, mlp_norm, final_norm.reshape(1, -1))
    late_weights, cast_late = (w_out[0], w_up[0], w_down[0]), True
    outs = []
    for x in (x_prompt, x_sample):
        cos, sin = _rope_tables(x.shape[1], head_dim)
        y, late_weights = _trunk(x, *params, cos, sin, late_weights, cast_late)
        cast_late = False
        outs.append(y)
    return tuple(outs)
```

```python
import functools

import jax
import jax.numpy as jnp
from jax import lax
from jax.experimental import pallas as pl
from jax.experimental.pallas import tpu as pltpu

EPS = 1e-6
GRID_W = 64
ROPE_THETA = 10000.0
N_KV_HEADS = 4
CONV_GROUPS = 16
LOG2_E = 1.4426950408889634

V7X_VMEM_BYTES = 64 * 1024 * 1024
VMEM_LIMIT_BYTES = V7X_VMEM_BYTES - 2 * 1024 * 1024

F32 = jnp.float32
BF16 = jnp.bfloat16
BF16_SUBLANES = 16


def _largest_tile(dim, target, quantum):
    t = min(target, dim)
    t -= t % quantum
    while t > quantum and dim % t:
        t -= quantum
    assert t >= quantum and dim % t == 0, (dim, target, quantum)
    return t


def _rope_tables(seq_len, head_dim):
    axis_dim = head_dim // 2
    n_rows = seq_len // GRID_W
    row = jnp.repeat(jnp.arange(n_rows, dtype=F32), GRID_W)
    col = jnp.tile(jnp.arange(GRID_W, dtype=F32), n_rows)
    freqs = ROPE_THETA ** (-jnp.arange(0, axis_dim, 2, dtype=F32) / axis_dim)
    ang_r = row[:, None] * freqs
    ang_c = col[:, None] * freqs
    cos = jnp.concatenate([jnp.cos(ang_r), jnp.cos(ang_r), jnp.cos(ang_c), jnp.cos(ang_c)], axis=-1)
    sin = jnp.concatenate([-jnp.sin(ang_r), jnp.sin(ang_r), -jnp.sin(ang_c), jnp.sin(ang_c)], axis=-1)
    return cos, sin


def _rms_scale(x):
    return lax.rsqrt(jnp.mean(x * x, axis=-1, keepdims=True) + EPS)


def _in_proj_kernel(x_ref, g_ref, w_ref, z_ref, h_ref, r_ref):
    @pl.when(pl.program_id(1) == 0)
    def _():
        xf = x_ref[...]
        h_ref[...] = (xf * g_ref[...]).astype(BF16)
        r_ref[...] = _rms_scale(xf)

    z_ref[...] = (jnp.dot(h_ref[...], w_ref[...], preferred_element_type=F32) * r_ref[...]).astype(BF16)


def _in_proj(x2, mix_g, w_in):
    m, d = x2.shape
    n = w_in.shape[1]
    tm = _largest_tile(m, 512, 16)
    tn = _largest_tile(n, 1024, 128)
    return pl.pallas_call(
        _in_proj_kernel,
        grid=(m // tm, n // tn),
        in_specs=[
            pl.BlockSpec((tm, d), lambda i, j: (i, 0)),
            pl.BlockSpec((1, d), lambda i, j: (0, 0)),
            pl.BlockSpec((d, tn), lambda i, j: (0, j)),
        ],
        out_specs=pl.BlockSpec((tm, tn), lambda i, j: (i, j)),
        out_shape=jax.ShapeDtypeStruct((m, n), BF16),
        scratch_shapes=[pltpu.VMEM((tm, d), BF16), pltpu.VMEM((tm, 1), F32)],
        compiler_params=pltpu.CompilerParams(
            dimension_semantics=("parallel", "arbitrary"), vmem_limit_bytes=VMEM_LIMIT_BYTES),
        name="in_proj",
    )(x2, mix_g, w_in)


def _norm_rope(a, gain, cos, sin):
    quarter = a.shape[1] // 4
    a = a * _rms_scale(a) * gain
    lane = lax.broadcasted_iota(jnp.int32, a.shape, 1)
    up = pltpu.roll(a, a.shape[1] - quarter, axis=1)
    down = pltpu.roll(a, quarter, axis=1)
    return a * cos + jnp.where((lane & quarter) == 0, up, down) * sin


def _attention_kernel(q_ref, k_ref, v_ref, qg_ref, kg_ref, cos_ref, sin_ref, g_ref, *refs,
                      group, head_dim, unit_rows, ramp_rows, n_cast):
    cast_in, (o_ref, *cast_out), (kt_ref, vx_ref) = refs[:n_cast], refs[n_cast:2 * n_cast + 1], refs[2 * n_cast + 1:]
    seq_len = k_ref.shape[0]
    tq = q_ref.shape[0]
    i = pl.program_id(2)

    @pl.when(i == 0)
    def _():
        k = _norm_rope(k_ref[...].astype(F32), kg_ref[...], cos_ref[...], sin_ref[...])
        kt_ref[...] = k.astype(BF16).T
        vx_ref[:, :head_dim] = v_ref[...]
        vx_ref[:, head_dim:] = jnp.ones((seq_len, head_dim), BF16)

    q_scale = float(head_dim) ** -0.5 * LOG2_E
    for h in range(group):
        cols = slice(h * head_dim, (h + 1) * head_dim)
        bounds = list(range(0, tq + 1, unit_rows))
        if h == 0 and unit_rows > ramp_rows:
            bounds.insert(1, ramp_rows)
        if h == group - 1 and unit_rows > ramp_rows:
            bounds.insert(-1, tq - ramp_rows)
        for r, r_end in zip(bounds[:-1], bounds[1:]):
            rows = slice(r, r_end)
            pos = pl.ds(pl.multiple_of(i * tq + r, ramp_rows), r_end - r)
            q = _norm_rope(q_ref[rows, cols].astype(F32), qg_ref[...], cos_ref[pos, :], sin_ref[pos, :])
            s = jnp.dot((q * q_scale).astype(BF16), kt_ref[...], preferred_element_type=F32)
            m = jnp.max(s, axis=-1, keepdims=True)
            p = jnp.exp2(s - m).astype(BF16)
            ol = jnp.dot(p, vx_ref[...], preferred_element_type=F32)
            o = ol[:, :head_dim] / ol[:, head_dim:]
            o_ref[rows, cols] = (o * _rms_scale(o) * g_ref[:, cols]).astype(BF16)

    for w_in_ref, w_out_ref in zip(cast_in, cast_out):
        w_out_ref[...] = w_in_ref[...].astype(BF16)


def _attention(z, q_g, k_g, cos, sin, attn_g, cast_weights, *, n_batch, seq_len, attn_width, kv_width):
    m = z.shape[0]
    head_dim = q_g.shape[-1]
    n_kv = N_KV_HEADS
    group = attn_width // head_dim // n_kv
    assert kv_width == n_kv * head_dim
    tq = _largest_tile(seq_len, 512, 16)
    q_blocks = seq_len // tq
    gw = group * head_dim
    k_col0 = attn_width // head_dim
    v_col0 = (attn_width + kv_width) // head_dim
    n_steps = n_batch * n_kv * q_blocks
    kern = functools.partial(_attention_kernel, group=group, head_dim=head_dim,
                             unit_rows=_largest_tile(tq, 256, 16), ramp_rows=_largest_tile(tq, 64, 16),
                             n_cast=len(cast_weights))
    whole = lambda b, h, i: (0, 0)
    q_map = lambda b, h, i: (b * q_blocks + i, h)
    step_map = lambda b, h, i: ((b * n_kv + h) * q_blocks + i, 0)
    cast_specs = []
    for w in cast_weights:
        assert w.shape[0] % (n_steps * BF16_SUBLANES) == 0, (w.shape, n_steps)
        cast_specs.append(pl.BlockSpec((w.shape[0] // n_steps, w.shape[1]), step_map))
    return pl.pallas_call(
        kern,
        grid=(n_batch, n_kv, q_blocks),
        in_specs=[
            pl.BlockSpec((tq, gw), q_map),
            pl.BlockSpec((seq_len, head_dim), lambda b, h, i: (b, k_col0 + h)),
            pl.BlockSpec((seq_len, head_dim), lambda b, h, i: (b, v_col0 + h)),
            pl.BlockSpec((1, head_dim), whole),
            pl.BlockSpec((1, head_dim), whole),
            pl.BlockSpec((seq_len, head_dim), whole),
            pl.BlockSpec((seq_len, head_dim), whole),
            pl.BlockSpec((1, gw), lambda b, h, i: (0, h)),
            *cast_specs,
        ],
        out_specs=[pl.BlockSpec((tq, gw), q_map), *cast_specs],
        out_shape=[jax.ShapeDtypeStruct((m, attn_width), BF16),
                   *(jax.ShapeDtypeStruct(w.shape, BF16) for w in cast_weights)],
        scratch_shapes=[pltpu.VMEM((head_dim, seq_len), BF16), pltpu.VMEM((seq_len, 2 * head_dim), BF16)],
        compiler_params=pltpu.CompilerParams(
            dimension_semantics=("arbitrary", "arbitrary", "arbitrary"),
            vmem_limit_bytes=VMEM_LIMIT_BYTES),
        name="attention",
    )(z, z, z, q_g, k_g, cos, sin, attn_g, *cast_weights)


def _conv_kernel(b_ref, c_ref, u_ref, w_ref, g_ref, o_ref):
    s = b_ref.shape[0]
    cu = c_ref[...].astype(F32) * u_ref[...].astype(F32)
    t = lax.broadcasted_iota(jnp.int32, cu.shape, 0)
    prev = jnp.where(t == 0, 0.0, pltpu.roll(cu, 1, axis=0))
    nxt = jnp.where(t == s - 1, 0.0, pltpu.roll(cu, s - 1, axis=0))
    w = w_ref[...]
    conv = w[0:1] * prev + w[1:2] * cu + w[2:3] * nxt
    c = b_ref[...].astype(F32) * conv
    o_ref[...] = (c * _rms_scale(c) * g_ref[...]).astype(BF16)


def _conv(z, conv_w, conv_g, *, n_batch, seq_len, attn_width, kv_width, conv_width, group_dim):
    m = z.shape[0]
    tc = group_dim
    col0 = (attn_width + 2 * kv_width) // tc
    n_c = conv_width // tc
    return pl.pallas_call(
        _conv_kernel,
        grid=(n_batch, n_c),
        in_specs=[
            pl.BlockSpec((seq_len, tc), lambda b, c: (b, col0 + c)),
            pl.BlockSpec((seq_len, tc), lambda b, c: (b, col0 + n_c + c)),
            pl.BlockSpec((seq_len, tc), lambda b, c: (b, col0 + 2 * n_c + c)),
            pl.BlockSpec((conv_w.shape[0], tc), lambda b, c: (0, c)),
            pl.BlockSpec((1, tc), lambda b, c: (0, c)),
        ],
        out_specs=pl.BlockSpec((seq_len, tc), lambda b, c: (b, c)),
        out_shape=jax.ShapeDtypeStruct((m, conv_width), BF16),
        compiler_params=pltpu.CompilerParams(
            dimension_semantics=("parallel", "parallel"), vmem_limit_bytes=VMEM_LIMIT_BYTES),
        name="gated_conv",
    )(z, z, z, conv_w, conv_g)


def _out_proj_kernel(x_ref, a_ref, c_ref, wa_ref, wc_ref, g_ref, o_ref, h_ref, ss_ref):
    acc = jnp.dot(a_ref[...], wa_ref[...], preferred_element_type=F32)
    acc += jnp.dot(c_ref[...], wc_ref[...], preferred_element_type=F32)
    x1 = x_ref[...] + acc
    o_ref[...] = x1
    h_ref[...] = (x1 * g_ref[...]).astype(BF16)
    part = jnp.sum(x1 * x1, axis=-1, keepdims=True)

    @pl.when(pl.program_id(1) == 0)
    def _():
        ss_ref[...] = part

    @pl.when(pl.program_id(1) > 0)
    def _():
        ss_ref[...] += part


def _out_proj(x2, a, c, w_out, mlp_g):
    m, d = x2.shape
    ka = a.shape[1]
    kc = c.shape[1]
    assert ka == kc and w_out.shape[0] == ka + kc
    tm = _largest_tile(m, 1024, 16)
    tn = _largest_tile(d, 1024, 128)
    return pl.pallas_call(
        _out_proj_kernel,
        grid=(m // tm, d // tn),
        in_specs=[
            pl.BlockSpec((tm, tn), lambda i, j: (i, j)),
            pl.BlockSpec((tm, ka), lambda i, j: (i, 0)),
            pl.BlockSpec((tm, kc), lambda i, j: (i, 0)),
            pl.BlockSpec((ka, tn), lambda i, j: (0, j)),
            pl.BlockSpec((kc, tn), lambda i, j: (1, j)),
            pl.BlockSpec((1, tn), lambda i, j: (0, j)),
        ],
        out_specs=[
            pl.BlockSpec((tm, tn), lambda i, j: (i, j)),
            pl.BlockSpec((tm, tn), lambda i, j: (i, j)),
            pl.BlockSpec((tm, 1), lambda i, j: (i, 0)),
        ],
        out_shape=[jax.ShapeDtypeStruct((m, d), F32), jax.ShapeDtypeStruct((m, d), BF16),
                   jax.ShapeDtypeStruct((m, 1), F32)],
        compiler_params=pltpu.CompilerParams(
            dimension_semantics=("parallel", "arbitrary"), vmem_limit_bytes=VMEM_LIMIT_BYTES),
        name="out_proj",
    )(x2, a, c, w_out, w_out, mlp_g)


def _mlp_kernel(x_ref, h_ref, ss_ref, wu_ref, wd_ref, fg_ref, o_ref, *, slab_cols):
    f = pl.program_id(1)
    last = pl.num_programs(1) - 1

    def act():
        u = jnp.dot(h_ref[...], wu_ref[...], preferred_element_type=F32)
        return jnp.square(jnp.maximum(u, 0.0)).astype(BF16)

    def down(a, cols):
        return jnp.dot(a, wd_ref[:, cols], preferred_element_type=F32)

    d = o_ref.shape[1]
    slabs = [slice(n, n + slab_cols) for n in range(0, d, slab_cols)]

    @pl.when(f == 0)
    def _():
        a = act()
        for cols in slabs:
            o_ref[:, cols] = down(a, cols)

    @pl.when((f > 0) & (f < last))
    def _():
        a = act()
        for cols in slabs:
            o_ref[:, cols] += down(a, cols)

    @pl.when(f == last)
    def _():
        a = act()
        r2 = 1.0 / (ss_ref[...] * (1.0 / d) + EPS)
        for cols in slabs:
            o_ref[:, cols] = x_ref[:, cols] + r2 * (o_ref[:, cols] + down(a, cols))
        y = o_ref[...]
        o_ref[...] = y * _rms_scale(y) * fg_ref[...]


def _mlp(x1, h, ss, w_up, w_down, final_g):
    m, d = x1.shape
    d_ff = w_up.shape[1]
    tm = _largest_tile(m, 512, 16)
    tf = _largest_tile(d_ff, 512, 128)
    assert d_ff // tf >= 2
    return pl.pallas_call(
        functools.partial(_mlp_kernel, slab_cols=_largest_tile(d, 1024, 128)),
        grid=(m // tm, d_ff // tf),
        in_specs=[
            pl.BlockSpec((tm, d), lambda i, f: (i, 0)),
            pl.BlockSpec((tm, d), lambda i, f: (i, 0)),
            pl.BlockSpec((tm, 1), lambda i, f: (i, 0)),
            pl.BlockSpec((d, tf), lambda i, f: (0, f)),
            pl.BlockSpec((tf, d), lambda i, f: (f, 0)),
            pl.BlockSpec((1, d), lambda i, f: (0, 0)),
        ],
        out_specs=pl.BlockSpec((tm, d), lambda i, f: (i, 0)),
        out_shape=jax.ShapeDtypeStruct((m, d), F32),
        compiler_params=pltpu.CompilerParams(
            dimension_semantics=("parallel", "arbitrary"), vmem_limit_bytes=VMEM_LIMIT_BYTES),
        name="mlp",
    )(x1, h, ss, w_up, w_down, final_g)


def _trunk(x, w_in, q_g, k_g, conv_w, attn_g, conv_g, mix_g, mlp_g, final_g, cos, sin, late_weights, cast_late):
    n_batch, seq_len, d = x.shape
    attn_width = attn_g.shape[-1]
    conv_width = conv_g.shape[-1]
    kv_width = (w_in.shape[1] - attn_width - 3 * conv_width) // 2
    x2 = x.reshape(n_batch * seq_len, d)
    z = _in_proj(x2, mix_g, w_in)
    a, *cast = _attention(z, q_g, k_g, cos, sin, attn_g, late_weights if cast_late else (),
                          n_batch=n_batch, seq_len=seq_len, attn_width=attn_width, kv_width=kv_width)
    w_out, w_up, w_down = cast if cast_late else late_weights
    c = _conv(z, conv_w, conv_g, n_batch=n_batch, seq_len=seq_len, attn_width=attn_width,
              kv_width=kv_width, conv_width=conv_width, group_dim=conv_width // CONV_GROUPS)
    x1, h, ss = _out_proj(x2, a, c, w_out, mlp_g)
    y = _mlp(x1, h, ss, w_up, w_down, final_g)
    return y.reshape(n_batch, seq_len, d), (w_out, w_up, w_down)


def kernel(x_prompt, x_sample, w_in, q_norm, k_norm, conv_w, attn_grp_norm, conv_grp_norm, w_out,
           mix_norm, mlp_norm, w_up, w_down, final_norm):
    assert w_in.shape[0] == 1, "single-layer trunk"
    head_dim = q_norm.shape[-1]
    params = (w_in[0].astype(BF16), q_norm, k_norm, conv_w[0], attn_grp_norm, conv_grp_norm,
              mix_norm, mlp_norm, final_norm.reshape(1, -1))
    late_weights, cast_late = (w_out[0], w_up[0], w_down[0]), True
    outs = []
    for x in (x_prompt, x_sample):
        cos, sin = _rope_tables(x.shape[1], head_dim)
        y, late_weights = _trunk(x, *params, cos, sin, late_weights, cast_late)
        cast_late = False
        outs.append(y)
    return tuple(outs)
```

```python
import functools

import jax
import jax.numpy as jnp
from jax import lax
from jax.experimental import pallas as pl
from jax.experimental.pallas import tpu as pltpu

EPS = 1e-6
GRID_W = 64
ROPE_THETA = 10000.0
N_KV_HEADS = 4
CONV_GROUPS = 16
LOG2_E = 1.4426950408889634

V7X_VMEM_BYTES = 64 * 1024 * 1024
VMEM_LIMIT_BYTES = V7X_VMEM_BYTES - 2 * 1024 * 1024

F32 = jnp.float32
BF16 = jnp.bfloat16
BF16_SUBLANES = 16


def _largest_tile(dim, target, quantum):
    t = min(target, dim)
    t -= t % quantum
    while t > quantum and dim % t:
        t -= quantum
    assert t >= quantum and dim % t == 0, (dim, target, quantum)
    return t


def _rope_tables(seq_len, head_dim):
    axis_dim = head_dim // 2
    n_rows = seq_len // GRID_W
    row = jnp.repeat(jnp.arange(n_rows, dtype=F32), GRID_W)
    col = jnp.tile(jnp.arange(GRID_W, dtype=F32), n_rows)
    freqs = ROPE_THETA ** (-jnp.arange(0, axis_dim, 2, dtype=F32) / axis_dim)
    ang_r = row[:, None] * freqs
    ang_c = col[:, None] * freqs
    cos = jnp.concatenate([jnp.cos(ang_r), jnp.cos(ang_r), jnp.cos(ang_c), jnp.cos(ang_c)], axis=-1)
    sin = jnp.concatenate([-jnp.sin(ang_r), jnp.sin(ang_r), -jnp.sin(ang_c), jnp.sin(ang_c)], axis=-1)
    return cos, sin


def _rms_scale(x):
    return lax.rsqrt(jnp.mean(x * x, axis=-1, keepdims=True) + EPS)


def _in_proj_kernel(x_ref, g_ref, w_ref, *refs, n_cast):
    cast_in, (z_ref, *cast_out), (h_ref, r_ref) = refs[:n_cast], refs[n_cast:2 * n_cast + 1], refs[2 * n_cast + 1:]

    @pl.when(pl.program_id(1) == 0)
    def _():
        xf = x_ref[...]
        h_ref[...] = (xf * g_ref[...]).astype(BF16)
        r_ref[...] = _rms_scale(xf)

    z_ref[...] = (jnp.dot(h_ref[...], w_ref[...], preferred_element_type=F32) * r_ref[...]).astype(BF16)
    for w_in_ref, w_out_ref in zip(cast_in, cast_out):
        w_out_ref[...] = w_in_ref[...].astype(BF16)


def _in_proj(x2, mix_g, w_in, cast_weights):
    m, d = x2.shape
    n = w_in.shape[1]
    tm = _largest_tile(m, 512, 16)
    tn = _largest_tile(n, 1024, 128)
    n_i, n_j = m // tm, n // tn
    cast_per_tile = 1 << (n_j.bit_length() - 1)
    cast_map = lambda i, j: (i * cast_per_tile + jnp.minimum(j, cast_per_tile - 1), 0)
    cast_specs = []
    for w in cast_weights:
        assert w.shape[0] % (n_i * cast_per_tile * BF16_SUBLANES) == 0, (w.shape, n_i, cast_per_tile)
        cast_specs.append(pl.BlockSpec((w.shape[0] // (n_i * cast_per_tile), w.shape[1]), cast_map))
    return pl.pallas_call(
        functools.partial(_in_proj_kernel, n_cast=len(cast_weights)),
        grid=(n_i, n_j),
        in_specs=[
            pl.BlockSpec((tm, d), lambda i, j: (i, 0)),
            pl.BlockSpec((1, d), lambda i, j: (0, 0)),
            pl.BlockSpec((d, tn), lambda i, j: (0, j)),
            *cast_specs,
        ],
        out_specs=[pl.BlockSpec((tm, tn), lambda i, j: (i, j)), *cast_specs],
        out_shape=[jax.ShapeDtypeStruct((m, n), BF16),
                   *(jax.ShapeDtypeStruct(w.shape, BF16) for w in cast_weights)],
        scratch_shapes=[pltpu.VMEM((tm, d), BF16), pltpu.VMEM((tm, 1), F32)],
        compiler_params=pltpu.CompilerParams(
            dimension_semantics=("arbitrary", "arbitrary"), vmem_limit_bytes=VMEM_LIMIT_BYTES),
        name="in_proj",
    )(x2, mix_g, w_in, *cast_weights)


def _norm_rope(a, gain, cos, sin):
    quarter = a.shape[1] // 4
    a = a * _rms_scale(a) * gain
    lane = lax.broadcasted_iota(jnp.int32, a.shape, 1)
    up = pltpu.roll(a, a.shape[1] - quarter, axis=1)
    down = pltpu.roll(a, quarter, axis=1)
    return a * cos + jnp.where((lane & quarter) == 0, up, down) * sin


def _gated_conv(b, c, u, w, gain):
    s = b.shape[0]
    cu = c * u
    t = lax.broadcasted_iota(jnp.int32, cu.shape, 0)
    prev = jnp.where(t == 0, 0.0, pltpu.roll(cu, 1, axis=0))
    nxt = jnp.where(t == s - 1, 0.0, pltpu.roll(cu, s - 1, axis=0))
    y = b * (w[0:1] * prev + w[1:2] * cu + w[2:3] * nxt)
    return y * _rms_scale(y) * gain


def _attention_kernel(q_ref, k_ref, v_ref, qg_ref, kg_ref, cos_ref, sin_ref, g_ref,
                      cb_ref, cc_ref, cu_ref, cw_ref, cg_ref, o_ref, co_ref, kt_ref, vx_ref, *,
                      group, head_dim, unit_rows, ramp_rows, conv_group_dim):
    seq_len = k_ref.shape[0]
    tq = q_ref.shape[0]
    i = pl.program_id(2)

    @pl.when(i == 0)
    def _():
        k = _norm_rope(k_ref[...].astype(F32), kg_ref[...], cos_ref[...], sin_ref[...])
        kt_ref[...] = k.astype(BF16).T
        vx_ref[:, :head_dim] = v_ref[...]
        vx_ref[:, head_dim:] = jnp.ones((seq_len, head_dim), BF16)

    q_scale = float(head_dim) ** -0.5 * LOG2_E
    for h in range(group):
        cols = slice(h * head_dim, (h + 1) * head_dim)
        bounds = list(range(0, tq + 1, unit_rows))
        if h == 0 and unit_rows > ramp_rows:
            bounds.insert(1, ramp_rows)
        if h == group - 1 and unit_rows > ramp_rows:
            bounds.insert(-1, tq - ramp_rows)
        for r, r_end in zip(bounds[:-1], bounds[1:]):
            rows = slice(r, r_end)
            pos = pl.ds(pl.multiple_of(i * tq + r, ramp_rows), r_end - r)
            q = _norm_rope(q_ref[rows, cols].astype(F32), qg_ref[...], cos_ref[pos, :], sin_ref[pos, :])
            s = jnp.dot((q * q_scale).astype(BF16), kt_ref[...], preferred_element_type=F32)
            m = jnp.max(s, axis=-1, keepdims=True)
            p = jnp.exp2(s - m).astype(BF16)
            ol = jnp.dot(p, vx_ref[...], preferred_element_type=F32)
            o = ol[:, :head_dim] / ol[:, head_dim:]
            o_ref[rows, cols] = (o * _rms_scale(o) * g_ref[:, cols]).astype(BF16)

    for c0 in range(0, cb_ref.shape[1], conv_group_dim):
        cc = slice(c0, c0 + conv_group_dim)
        y = _gated_conv(cb_ref[:, cc].astype(F32), cc_ref[:, cc].astype(F32), cu_ref[:, cc].astype(F32),
                        cw_ref[:, cc], cg_ref[:, cc])
        co_ref[:, cc] = y.astype(BF16)


def _attention(z, q_g, k_g, cos, sin, attn_g, conv_w, conv_g, *, n_batch, seq_len, attn_width, kv_width):
    m = z.shape[0]
    head_dim = q_g.shape[-1]
    conv_width = conv_g.shape[-1]
    n_kv = N_KV_HEADS
    group = attn_width // head_dim // n_kv
    assert kv_width == n_kv * head_dim
    tq = _largest_tile(seq_len, 1024, 16)
    q_blocks = seq_len // tq
    gw = group * head_dim
    k_col0 = attn_width // head_dim
    v_col0 = (attn_width + kv_width) // head_dim
    steps_per_batch = n_kv * q_blocks
    conv_cols = conv_width // steps_per_batch
    conv_group_dim = conv_width // CONV_GROUPS
    conv_col0 = attn_width + 2 * kv_width
    assert conv_width % steps_per_batch == 0 and conv_cols % conv_group_dim == 0 and conv_col0 % conv_cols == 0
    n_conv = conv_width // conv_cols
    kern = functools.partial(_attention_kernel, group=group, head_dim=head_dim,
                             unit_rows=_largest_tile(tq, 256, 16), ramp_rows=_largest_tile(tq, 64, 16),
                             conv_group_dim=conv_group_dim)
    whole = lambda b, h, i: (0, 0)
    q_map = lambda b, h, i: (b * q_blocks + i, h)

    def conv_in(part):
        return pl.BlockSpec((seq_len, conv_cols),
                            lambda b, h, i: (b, conv_col0 // conv_cols + part * n_conv + h * q_blocks + i))

    conv_par = lambda b, h, i: (0, h * q_blocks + i)
    once = dict(pipeline_mode=pl.Buffered(1))
    return pl.pallas_call(
        kern,
        grid=(n_batch, n_kv, q_blocks),
        in_specs=[
            pl.BlockSpec((tq, gw), q_map),
            pl.BlockSpec((seq_len, head_dim), lambda b, h, i: (b, k_col0 + h)),
            pl.BlockSpec((seq_len, head_dim), lambda b, h, i: (b, v_col0 + h)),
            pl.BlockSpec((1, head_dim), whole),
            pl.BlockSpec((1, head_dim), whole),
            pl.BlockSpec((seq_len, head_dim), whole, **once),
            pl.BlockSpec((seq_len, head_dim), whole, **once),
            pl.BlockSpec((1, gw), lambda b, h, i: (0, h)),
            conv_in(0), conv_in(1), conv_in(2),
            pl.BlockSpec((conv_w.shape[0], conv_cols), conv_par),
            pl.BlockSpec((1, conv_cols), conv_par),
        ],
        out_specs=[pl.BlockSpec((tq, gw), q_map),
                   pl.BlockSpec((seq_len, conv_cols), lambda b, h, i: (b, h * q_blocks + i))],
        out_shape=[jax.ShapeDtypeStruct((m, attn_width), BF16), jax.ShapeDtypeStruct((m, conv_width), BF16)],
        scratch_shapes=[pltpu.VMEM((head_dim, seq_len), BF16), pltpu.VMEM((seq_len, 2 * head_dim), BF16)],
        compiler_params=pltpu.CompilerParams(
            dimension_semantics=("parallel", "parallel", "arbitrary"),
            vmem_limit_bytes=VMEM_LIMIT_BYTES),
        name="attention",
    )(z, z, z, q_g, k_g, cos, sin, attn_g, z, z, z, conv_w, conv_g)


def _out_proj_kernel(x_ref, a_ref, c_ref, wa_ref, wc_ref, g_ref, o_ref, h_ref, ss_ref):
    acc = jnp.dot(a_ref[...], wa_ref[...], preferred_element_type=F32)
    acc += jnp.dot(c_ref[...], wc_ref[...], preferred_element_type=F32)
    x1 = x_ref[...] + acc
    o_ref[...] = x1
    h_ref[...] = (x1 * g_ref[...]).astype(BF16)
    part = jnp.sum(x1 * x1, axis=-1, keepdims=True)

    @pl.when(pl.program_id(1) == 0)
    def _():
        ss_ref[...] = part

    @pl.when(pl.program_id(1) > 0)
    def _():
        ss_ref[...] += part


def _out_proj(x2, a, c, w_out, mlp_g):
    m, d = x2.shape
    ka = a.shape[1]
    kc = c.shape[1]
    assert ka == kc and w_out.shape[0] == ka + kc
    tm = _largest_tile(m, 1024, 16)
    tn = _largest_tile(d, 1024, 128)
    return pl.pallas_call(
        _out_proj_kernel,
        grid=(m // tm, d // tn),
        in_specs=[
            pl.BlockSpec((tm, tn), lambda i, j: (i, j)),
            pl.BlockSpec((tm, ka), lambda i, j: (i, 0)),
            pl.BlockSpec((tm, kc), lambda i, j: (i, 0)),
            pl.BlockSpec((ka, tn), lambda i, j: (0, j)),
            pl.BlockSpec((kc, tn), lambda i, j: (1, j)),
            pl.BlockSpec((1, tn), lambda i, j: (0, j)),
        ],
        out_specs=[
            pl.BlockSpec((tm, tn), lambda i, j: (i, j)),
            pl.BlockSpec((tm, tn), lambda i, j: (i, j)),
            pl.BlockSpec((tm, 1), lambda i, j: (i, 0)),
        ],
        out_shape=[jax.ShapeDtypeStruct((m, d), F32), jax.ShapeDtypeStruct((m, d), BF16),
                   jax.ShapeDtypeStruct((m, 1), F32)],
        compiler_params=pltpu.CompilerParams(
            dimension_semantics=("parallel", "arbitrary"), vmem_limit_bytes=VMEM_LIMIT_BYTES),
        name="out_proj",
    )(x2, a, c, w_out, w_out, mlp_g)


def _mlp_kernel(x_ref, h_ref, ss_ref, wu_ref, wd_ref, fg_ref, o_ref, *, slab_cols):
    f = pl.program_id(1)
    last = pl.num_programs(1) - 1

    def act():
        u = jnp.dot(h_ref[...], wu_ref[...], preferred_element_type=F32)
        return jnp.square(jnp.maximum(u, 0.0)).astype(BF16)

    def down(a, cols):
        return jnp.dot(a, wd_ref[:, cols], preferred_element_type=F32)

    d = o_ref.shape[1]
    slabs = [slice(n, n + slab_cols) for n in range(0, d, slab_cols)]

    @pl.when(f == 0)
    def _():
        a = act()
        for cols in slabs:
            o_ref[:, cols] = down(a, cols)

    @pl.when((f > 0) & (f < last))
    def _():
        a = act()
        for cols in slabs:
            o_ref[:, cols] += down(a, cols)

    @pl.when(f == last)
    def _():
        a = act()
        r2 = 1.0 / (ss_ref[...] * (1.0 / d) + EPS)
        for cols in slabs:
            o_ref[:, cols] = x_ref[:, cols] + r2 * (o_ref[:, cols] + down(a, cols))
        y = o_ref[...]
        o_ref[...] = y * _rms_scale(y) * fg_ref[...]


def _mlp(x1, h, ss, w_up, w_down, final_g):
    m, d = x1.shape
    d_ff = w_up.shape[1]
    tm = _largest_tile(m, 512, 16)
    tf = _largest_tile(d_ff, 512, 128)
    assert d_ff // tf >= 2
    return pl.pallas_call(
        functools.partial(_mlp_kernel, slab_cols=_largest_tile(d, 1024, 128)),
        grid=(m // tm, d_ff // tf),
        in_specs=[
            pl.BlockSpec((tm, d), lambda i, f: (i, 0)),
            pl.BlockSpec((tm, d), lambda i, f: (i, 0)),
            pl.BlockSpec((tm, 1), lambda i, f: (i, 0)),
            pl.BlockSpec((d, tf), lambda i, f: (0, f)),
            pl.BlockSpec((tf, d), lambda i, f: (f, 0)),
            pl.BlockSpec((1, d), lambda i, f: (0, 0)),
        ],
        out_specs=pl.BlockSpec((tm, d), lambda i, f: (i, 0)),
        out_shape=jax.ShapeDtypeStruct((m, d), F32),
        compiler_params=pltpu.CompilerParams(
            dimension_semantics=("parallel", "arbitrary"), vmem_limit_bytes=VMEM_LIMIT_BYTES),
        name="mlp",
    )(x1, h, ss, w_up, w_down, final_g)


def _trunk(x, w_in, q_g, k_g, conv_w, attn_g, conv_g, mix_g, mlp_g, final_g, cos, sin, late_weights, cast_late):
    n_batch, seq_len, d = x.shape
    attn_width = attn_g.shape[-1]
    conv_width = conv_g.shape[-1]
    kv_width = (w_in.shape[1] - attn_width - 3 * conv_width) // 2
    x2 = x.reshape(n_batch * seq_len, d)
    z, *cast = _in_proj(x2, mix_g, w_in, late_weights if cast_late else ())
    w_out, w_up, w_down = cast if cast_late else late_weights
    a, c = _attention(z, q_g, k_g, cos, sin, attn_g, conv_w, conv_g, n_batch=n_batch, seq_len=seq_len,
                      attn_width=attn_width, kv_width=kv_width)
    x1, h, ss = _out_proj(x2, a, c, w_out, mlp_g)
    y = _mlp(x1, h, ss, w_up, w_down, final_g)
    return y.reshape(n_batch, seq_len, d), (w_out, w_up, w_down)


def kernel(x_prompt, x_sample, w_in, q_norm, k_norm, conv_w, attn_grp_norm, conv_grp_norm, w_out,
           mix_norm, mlp_norm, w_up, w_down, final_norm):
    assert w_in.shape[0] == 1, "single-layer trunk"
    head_dim = q_norm.shape[-1]
    params = (w_in[0].astype(BF16), q_norm, k_norm, conv_w[0], attn_grp_norm, conv_grp_norm,
              mix_norm, mlp_norm, final_norm.reshape(1, -1))
    late_weights, cast_late = (w_out[0], w_up[0], w_down[0]), True
    outs = []
    for x in (x_prompt, x_sample):
        cos, sin = _rope_tables(x.shape[1], head_dim)
        y, late_weights = _trunk(x, *params, cos, sin, late_weights, cast_late)
        cast_late = False
        outs.append(y)
    return tuple(outs)
```

```python
import functools

import jax
import jax.numpy as jnp
from jax import lax
from jax.experimental import pallas as pl
from jax.experimental.pallas import tpu as pltpu

EPS = 1e-6
GRID_W = 64
ROPE_THETA = 10000.0
N_KV_HEADS = 4
CONV_GROUPS = 16
LOG2_E = 1.4426950408889634

V7X_VMEM_BYTES = 64 * 1024 * 1024
VMEM_LIMIT_BYTES = V7X_VMEM_BYTES - 2 * 1024 * 1024

F32 = jnp.float32
BF16 = jnp.bfloat16
BF16_SUBLANES = 16


def _largest_tile(dim, target, quantum):
    t = min(target, dim)
    t -= t % quantum
    while t > quantum and dim % t:
        t -= quantum
    assert t >= quantum and dim % t == 0, (dim, target, quantum)
    return t


def _rope_tables(seq_len, head_dim):
    axis_dim = head_dim // 2
    n_rows = seq_len // GRID_W
    row = jnp.repeat(jnp.arange(n_rows, dtype=F32), GRID_W)
    col = jnp.tile(jnp.arange(GRID_W, dtype=F32), n_rows)
    freqs = ROPE_THETA ** (-jnp.arange(0, axis_dim, 2, dtype=F32) / axis_dim)
    ang_r = row[:, None] * freqs
    ang_c = col[:, None] * freqs
    cos = jnp.concatenate([jnp.cos(ang_r), jnp.cos(ang_r), jnp.cos(ang_c), jnp.cos(ang_c)], axis=-1)
    sin = jnp.concatenate([-jnp.sin(ang_r), jnp.sin(ang_r), -jnp.sin(ang_c), jnp.sin(ang_c)], axis=-1)
    return cos, sin


def _rms_scale(x):
    return lax.rsqrt(jnp.mean(x * x, axis=-1, keepdims=True) + EPS)


def _in_proj_kernel(x_ref, g_ref, w_ref, z_ref, h_ref, r_ref):
    @pl.when(pl.program_id(1) == 0)
    def _():
        xf = x_ref[...]
        h_ref[...] = (xf * g_ref[...]).astype(BF16)
        r_ref[...] = _rms_scale(xf)

    z_ref[...] = (jnp.dot(h_ref[...], w_ref[...], preferred_element_type=F32) * r_ref[...]).astype(BF16)


def _in_proj(x2, mix_g, w_in):
    m, d = x2.shape
    n = w_in.shape[1]
    tm = _largest_tile(m, 512, 16)
    tn = _largest_tile(n, 1024, 128)
    return pl.pallas_call(
        _in_proj_kernel,
        grid=(m // tm, n // tn),
        in_specs=[
            pl.BlockSpec((tm, d), lambda i, j: (i, 0)),
            pl.BlockSpec((1, d), lambda i, j: (0, 0)),
            pl.BlockSpec((d, tn), lambda i, j: (0, j)),
        ],
        out_specs=pl.BlockSpec((tm, tn), lambda i, j: (i, j)),
        out_shape=jax.ShapeDtypeStruct((m, n), BF16),
        scratch_shapes=[pltpu.VMEM((tm, d), BF16), pltpu.VMEM((tm, 1), F32)],
        compiler_params=pltpu.CompilerParams(
            dimension_semantics=("parallel", "arbitrary"), vmem_limit_bytes=VMEM_LIMIT_BYTES),
        name="in_proj",
    )(x2, mix_g, w_in)


def _norm_rope(a, gain, cos, sin):
    quarter = a.shape[1] // 4
    a = a * _rms_scale(a) * gain
    lane = lax.broadcasted_iota(jnp.int32, a.shape, 1)
    up = pltpu.roll(a, a.shape[1] - quarter, axis=1)
    down = pltpu.roll(a, quarter, axis=1)
    return a * cos + jnp.where((lane & quarter) == 0, up, down) * sin


def _gated_conv(b, c, u, w, gain):
    s = b.shape[0]
    cu = c * u
    t = lax.broadcasted_iota(jnp.int32, cu.shape, 0)
    prev = jnp.where(t == 0, 0.0, pltpu.roll(cu, 1, axis=0))
    nxt = jnp.where(t == s - 1, 0.0, pltpu.roll(cu, s - 1, axis=0))
    y = b * (w[0:1] * prev + w[1:2] * cu + w[2:3] * nxt)
    return y * _rms_scale(y) * gain


def _attention_kernel(q_ref, k_ref, v_ref, qg_ref, kg_ref, cos_ref, sin_ref, g_ref,
                      cb_ref, cc_ref, cu_ref, cw_ref, cg_ref, *refs,
                      group, head_dim, unit_rows, ramp_rows, conv_group_dim, n_cast):
    cast_in, (o_ref, co_ref, *cast_out), (kt_ref, vx_ref) = (
        refs[:n_cast], refs[n_cast:2 * n_cast + 2], refs[2 * n_cast + 2:])
    seq_len = k_ref.shape[0]
    tq = q_ref.shape[0]
    i = pl.program_id(2)

    @pl.when(i == 0)
    def _():
        k = _norm_rope(k_ref[...].astype(F32), kg_ref[...], cos_ref[...], sin_ref[...])
        kt_ref[...] = k.astype(BF16).T
        vx_ref[:, :head_dim] = v_ref[...]
        vx_ref[:, head_dim:] = jnp.ones((seq_len, head_dim), BF16)

    q_scale = float(head_dim) ** -0.5 * LOG2_E
    for h in range(group):
        cols = slice(h * head_dim, (h + 1) * head_dim)
        bounds = list(range(0, tq + 1, unit_rows))
        if h == 0 and unit_rows > ramp_rows:
            bounds.insert(1, ramp_rows)
        if h == group - 1 and unit_rows > ramp_rows:
            bounds.insert(-1, tq - ramp_rows)
        for r, r_end in zip(bounds[:-1], bounds[1:]):
            rows = slice(r, r_end)
            pos = pl.ds(pl.multiple_of(i * tq + r, ramp_rows), r_end - r)
            q = _norm_rope(q_ref[rows, cols].astype(F32), qg_ref[...], cos_ref[pos, :], sin_ref[pos, :])
            s = jnp.dot((q * q_scale).astype(BF16), kt_ref[...], preferred_element_type=F32)
            m = jnp.max(s, axis=-1, keepdims=True)
            p = jnp.exp2(s - m).astype(BF16)
            ol = jnp.dot(p, vx_ref[...], preferred_element_type=F32)
            o = ol[:, :head_dim] / ol[:, head_dim:]
            o_ref[rows, cols] = (o * _rms_scale(o) * g_ref[:, cols]).astype(BF16)

    for c0 in range(0, cb_ref.shape[1], conv_group_dim):
        cc = slice(c0, c0 + conv_group_dim)
        y = _gated_conv(cb_ref[:, cc].astype(F32), cc_ref[:, cc].astype(F32), cu_ref[:, cc].astype(F32),
                        cw_ref[:, cc], cg_ref[:, cc])
        co_ref[:, cc] = y.astype(BF16)

    for w_in_ref, w_out_ref in zip(cast_in, cast_out):
        w_out_ref[...] = w_in_ref[...].astype(BF16)


def _attention(z, q_g, k_g, cos, sin, attn_g, conv_w, conv_g, cast_weights, *,
               n_batch, seq_len, attn_width, kv_width):
    m = z.shape[0]
    head_dim = q_g.shape[-1]
    conv_width = conv_g.shape[-1]
    n_kv = N_KV_HEADS
    group = attn_width // head_dim // n_kv
    assert kv_width == n_kv * head_dim
    tq = _largest_tile(seq_len, 1024, 16)
    q_blocks = seq_len // tq
    gw = group * head_dim
    k_col0 = attn_width // head_dim
    v_col0 = (attn_width + kv_width) // head_dim
    steps_per_batch = n_kv * q_blocks
    conv_cols = conv_width // steps_per_batch
    conv_group_dim = conv_width // CONV_GROUPS
    conv_col0 = attn_width + 2 * kv_width
    assert conv_width % steps_per_batch == 0 and conv_cols % conv_group_dim == 0 and conv_col0 % conv_cols == 0
    n_conv = conv_width // conv_cols
    kern = functools.partial(_attention_kernel, group=group, head_dim=head_dim,
                             unit_rows=_largest_tile(tq, 256, 16), ramp_rows=_largest_tile(tq, 64, 16),
                             conv_group_dim=conv_group_dim, n_cast=len(cast_weights))
    n_steps = n_batch * steps_per_batch
    step_map = lambda b, h, i: ((b * n_kv + h) * q_blocks + i, 0)
    cast_specs = []
    for w in cast_weights:
        assert w.shape[0] % (n_steps * BF16_SUBLANES) == 0, (w.shape, n_steps)
        cast_specs.append(pl.BlockSpec((w.shape[0] // n_steps, w.shape[1]), step_map))
    whole = lambda b, h, i: (0, 0)
    q_map = lambda b, h, i: (b * q_blocks + i, h)

    def conv_in(part):
        return pl.BlockSpec((seq_len, conv_cols),
                            lambda b, h, i: (b, conv_col0 // conv_cols + part * n_conv + h * q_blocks + i))

    conv_par = lambda b, h, i: (0, h * q_blocks + i)
    once = dict(pipeline_mode=pl.Buffered(1))
    return pl.pallas_call(
        kern,
        grid=(n_batch, n_kv, q_blocks),
        in_specs=[
            pl.BlockSpec((tq, gw), q_map),
            pl.BlockSpec((seq_len, head_dim), lambda b, h, i: (b, k_col0 + h), **once),
            pl.BlockSpec((seq_len, head_dim), lambda b, h, i: (b, v_col0 + h), **once),
            pl.BlockSpec((1, head_dim), whole),
            pl.BlockSpec((1, head_dim), whole),
            pl.BlockSpec((seq_len, head_dim), whole, **once),
            pl.BlockSpec((seq_len, head_dim), whole, **once),
            pl.BlockSpec((1, gw), lambda b, h, i: (0, h)),
            conv_in(0), conv_in(1), conv_in(2),
            pl.BlockSpec((conv_w.shape[0], conv_cols), conv_par),
            pl.BlockSpec((1, conv_cols), conv_par),
            *cast_specs,
        ],
        out_specs=[pl.BlockSpec((tq, gw), q_map),
                   pl.BlockSpec((seq_len, conv_cols), lambda b, h, i: (b, h * q_blocks + i)),
                   *cast_specs],
        out_shape=[jax.ShapeDtypeStruct((m, attn_width), BF16), jax.ShapeDtypeStruct((m, conv_width), BF16),
                   *(jax.ShapeDtypeStruct(w.shape, BF16) for w in cast_weights)],
        scratch_shapes=[pltpu.VMEM((head_dim, seq_len), BF16), pltpu.VMEM((seq_len, 2 * head_dim), BF16)],
        compiler_params=pltpu.CompilerParams(
            dimension_semantics=("arbitrary", "arbitrary", "arbitrary"),
            vmem_limit_bytes=VMEM_LIMIT_BYTES),
        name="attention",
    )(z, z, z, q_g, k_g, cos, sin, attn_g, z, z, z, conv_w, conv_g, *cast_weights)


def _out_proj_kernel(x_ref, a_ref, c_ref, wa_ref, wc_ref, g_ref, o_ref, h_ref, ss_ref):
    acc = jnp.dot(a_ref[...], wa_ref[...], preferred_element_type=F32)
    acc += jnp.dot(c_ref[...], wc_ref[...], preferred_element_type=F32)
    x1 = x_ref[...] + acc
    o_ref[...] = x1
    h_ref[...] = (x1 * g_ref[...]).astype(BF16)
    part = jnp.sum(x1 * x1, axis=-1, keepdims=True)

    @pl.when(pl.program_id(1) == 0)
    def _():
        ss_ref[...] = part

    @pl.when(pl.program_id(1) > 0)
    def _():
        ss_ref[...] += part


def _out_proj(x2, a, c, w_out, mlp_g):
    m, d = x2.shape
    ka = a.shape[1]
    kc = c.shape[1]
    assert ka == kc and w_out.shape[0] == ka + kc
    tm = _largest_tile(m, 1024, 16)
    tn = _largest_tile(d, 1024, 128)
    return pl.pallas_call(
        _out_proj_kernel,
        grid=(m // tm, d // tn),
        in_specs=[
            pl.BlockSpec((tm, tn), lambda i, j: (i, j)),
            pl.BlockSpec((tm, ka), lambda i, j: (i, 0)),
            pl.BlockSpec((tm, kc), lambda i, j: (i, 0)),
            pl.BlockSpec((ka, tn), lambda i, j: (0, j)),
            pl.BlockSpec((kc, tn), lambda i, j: (1, j)),
            pl.BlockSpec((1, tn), lambda i, j: (0, j)),
        ],
        out_specs=[
            pl.BlockSpec((tm, tn), lambda i, j: (i, j)),
            pl.BlockSpec((tm, tn), lambda i, j: (i, j)),
            pl.BlockSpec((tm, 1), lambda i, j: (i, 0)),
        ],
        out_shape=[jax.ShapeDtypeStruct((m, d), F32), jax.ShapeDtypeStruct((m, d), BF16),
                   jax.ShapeDtypeStruct((m, 1), F32)],
        compiler_params=pltpu.CompilerParams(
            dimension_semantics=("parallel", "arbitrary"), vmem_limit_bytes=VMEM_LIMIT_BYTES),
        name="out_proj",
    )(x2, a, c, w_out, w_out, mlp_g)


def _mlp_kernel(x_ref, h_ref, ss_ref, wu_ref, wd_ref, fg_ref, o_ref, *, slab_cols):
    f = pl.program_id(1)
    last = pl.num_programs(1) - 1

    def act():
        u = jnp.dot(h_ref[...], wu_ref[...], preferred_element_type=F32)
        return jnp.square(jnp.maximum(u, 0.0)).astype(BF16)

    def down(a, cols):
        return jnp.dot(a, wd_ref[:, cols], preferred_element_type=F32)

    d = o_ref.shape[1]
    slabs = [slice(n, n + slab_cols) for n in range(0, d, slab_cols)]

    @pl.when(f == 0)
    def _():
        a = act()
        for cols in slabs:
            o_ref[:, cols] = down(a, cols)

    @pl.when((f > 0) & (f < last))
    def _():
        a = act()
        for cols in slabs:
            o_ref[:, cols] += down(a, cols)

    @pl.when(f == last)
    def _():
        a = act()
        r2 = 1.0 / (ss_ref[...] * (1.0 / d) + EPS)
        for cols in slabs:
            o_ref[:, cols] = x_ref[:, cols] + r2 * (o_ref[:, cols] + down(a, cols))
        y = o_ref[...]
        o_ref[...] = y * _rms_scale(y) * fg_ref[...]


def _mlp(x1, h, ss, w_up, w_down, final_g):
    m, d = x1.shape
    d_ff = w_up.shape[1]
    tm = _largest_tile(m, 512, 16)
    tf = _largest_tile(d_ff, 512, 128)
    assert d_ff // tf >= 2
    return pl.pallas_call(
        functools.partial(_mlp_kernel, slab_cols=_largest_tile(d, 1024, 128)),
        grid=(m // tm, d_ff // tf),
        in_specs=[
            pl.BlockSpec((tm, d), lambda i, f: (i, 0)),
            pl.BlockSpec((tm, d), lambda i, f: (i, 0)),
            pl.BlockSpec((tm, 1), lambda i, f: (i, 0)),
            pl.BlockSpec((d, tf), lambda i, f: (0, f)),
            pl.BlockSpec((tf, d), lambda i, f: (f, 0)),
            pl.BlockSpec((1, d), lambda i, f: (0, 0)),
        ],
        out_specs=pl.BlockSpec((tm, d), lambda i, f: (i, 0)),
        out_shape=jax.ShapeDtypeStruct((m, d), F32),
        compiler_params=pltpu.CompilerParams(
            dimension_semantics=("parallel", "arbitrary"), vmem_limit_bytes=VMEM_LIMIT_BYTES),
        name="mlp",
    )(x1, h, ss, w_up, w_down, final_g)


def _trunk(x, w_in, q_g, k_g, conv_w, attn_g, conv_g, mix_g, mlp_g, final_g, cos, sin, late_weights, cast_late):
    n_batch, seq_len, d = x.shape
    attn_width = attn_g.shape[-1]
    conv_width = conv_g.shape[-1]
    kv_width = (w_in.shape[1] - attn_width - 3 * conv_width) // 2
    x2 = x.reshape(n_batch * seq_len, d)
    z = _in_proj(x2, mix_g, w_in)
    a, c, *cast = _attention(z, q_g, k_g, cos, sin, attn_g, conv_w, conv_g, late_weights if cast_late else (),
                             n_batch=n_batch, seq_len=seq_len, attn_width=attn_width, kv_width=kv_width)
    w_out, w_up, w_down = cast if cast_late else late_weights
    x1, h, ss = _out_proj(x2, a, c, w_out, mlp_g)
    y = _mlp(x1, h, ss, w_up, w_down, final_g)
    return y.reshape(n_batch, seq_len, d), (w_out, w_up, w_down)


def kernel(x_prompt, x_sample, w_in, q_norm, k_norm, conv_w, attn_grp_norm, conv_grp_norm, w_out,
           mix_norm, mlp_norm, w_up, w_down, final_norm):
    assert w_in.shape[0] == 1, "single-layer trunk"
    head_dim = q_norm.shape[-1]
    params = (w_in[0].astype(BF16), q_norm, k_norm, conv_w[0], attn_grp_norm, conv_grp_norm,
              mix_norm, mlp_norm, final_norm.reshape(1, -1))
    late_weights, cast_late = (w_out[0], w_up[0], w_down[0]), True
    outs = []
    for x in (x_prompt, x_sample):
        cos, sin = _rope_tables(x.shape[1], head_dim)
        y, late_weights = _trunk(x, *params, cos, sin, late_weights, cast_late)
        cast_late = False
        outs.append(y)
    return tuple(outs)
```

```python
import functools

import jax
import jax.numpy as jnp
from jax import lax
from jax.experimental import pallas as pl
from jax.experimental.pallas import tpu as pltpu

EPS = 1e-6
GRID_W = 64
ROPE_THETA = 10000.0
N_KV_HEADS = 4
CONV_GROUPS = 16
LOG2_E = 1.4426950408889634

V7X_VMEM_BYTES = 64 * 1024 * 1024
VMEM_LIMIT_BYTES = V7X_VMEM_BYTES - 2 * 1024 * 1024

F32 = jnp.float32
BF16 = jnp.bfloat16
BF16_SUBLANES = 16


def _largest_tile(dim, target, quantum):
    t = min(target, dim)
    t -= t % quantum
    while t > quantum and dim % t:
        t -= quantum
    assert t >= quantum and dim % t == 0, (dim, target, quantum)
    return t


def _rope_tables(seq_len, head_dim):
    axis_dim = head_dim // 2
    n_rows = seq_len // GRID_W
    row = jnp.repeat(jnp.arange(n_rows, dtype=F32), GRID_W)
    col = jnp.tile(jnp.arange(GRID_W, dtype=F32), n_rows)
    freqs = ROPE_THETA ** (-jnp.arange(0, axis_dim, 2, dtype=F32) / axis_dim)
    ang_r = row[:, None] * freqs
    ang_c = col[:, None] * freqs
    cos = jnp.concatenate([jnp.cos(ang_r), jnp.cos(ang_r), jnp.cos(ang_c), jnp.cos(ang_c)], axis=-1)
    sin = jnp.concatenate([-jnp.sin(ang_r), jnp.sin(ang_r), -jnp.sin(ang_c), jnp.sin(ang_c)], axis=-1)
    return cos, sin


def _rms_scale(x):
    return lax.rsqrt(jnp.mean(x * x, axis=-1, keepdims=True) + EPS)


def _in_proj_kernel(x_ref, g_ref, w_ref, *refs, n_cast):
    cast_in, (z_ref, *cast_out), (h_ref, r_ref) = refs[:n_cast], refs[n_cast:2 * n_cast + 1], refs[2 * n_cast + 1:]

    @pl.when(pl.program_id(1) == 0)
    def _():
        xf = x_ref[...]
        h_ref[...] = (xf * g_ref[...]).astype(BF16)
        r_ref[...] = _rms_scale(xf)

    z_ref[...] = (jnp.dot(h_ref[...], w_ref[...], preferred_element_type=F32) * r_ref[...]).astype(BF16)
    for w_in_ref, w_out_ref in zip(cast_in, cast_out):
        w_out_ref[...] = w_in_ref[...].astype(BF16)


def _in_proj(x2, mix_g, w_in, cast_weights):
    m, d = x2.shape
    n = w_in.shape[1]
    tm = _largest_tile(m, 512, 16)
    tn = _largest_tile(n, 1024, 128)
    n_i, n_j = m // tm, n // tn
    cast_per_tile = 1 << (n_j.bit_length() - 1)
    cast_map = lambda i, j: (i * cast_per_tile + jnp.minimum(j, cast_per_tile - 1), 0)
    cast_specs = []
    for w in cast_weights:
        assert w.shape[0] % (n_i * cast_per_tile * BF16_SUBLANES) == 0, (w.shape, n_i, cast_per_tile)
        cast_specs.append(pl.BlockSpec((w.shape[0] // (n_i * cast_per_tile), w.shape[1]), cast_map))
    return pl.pallas_call(
        functools.partial(_in_proj_kernel, n_cast=len(cast_weights)),
        grid=(n_i, n_j),
        in_specs=[
            pl.BlockSpec((tm, d), lambda i, j: (i, 0)),
            pl.BlockSpec((1, d), lambda i, j: (0, 0)),
            pl.BlockSpec((d, tn), lambda i, j: (0, j)),
            *cast_specs,
        ],
        out_specs=[pl.BlockSpec((tm, tn), lambda i, j: (i, j)), *cast_specs],
        out_shape=[jax.ShapeDtypeStruct((m, n), BF16),
                   *(jax.ShapeDtypeStruct(w.shape, BF16) for w in cast_weights)],
        scratch_shapes=[pltpu.VMEM((tm, d), BF16), pltpu.VMEM((tm, 1), F32)],
        compiler_params=pltpu.CompilerParams(
            dimension_semantics=("arbitrary", "arbitrary"), vmem_limit_bytes=VMEM_LIMIT_BYTES),
        name="in_proj",
    )(x2, mix_g, w_in, *cast_weights)


def _norm_rope(a, gain, cos, sin):
    quarter = a.shape[1] // 4
    a = a * _rms_scale(a) * gain
    lane = lax.broadcasted_iota(jnp.int32, a.shape, 1)
    up = pltpu.roll(a, a.shape[1] - quarter, axis=1)
    down = pltpu.roll(a, quarter, axis=1)
    return a * cos + jnp.where((lane & quarter) == 0, up, down) * sin


def _gated_conv(b, c, u, w, gain):
    s = b.shape[0]
    cu = c * u
    t = lax.broadcasted_iota(jnp.int32, cu.shape, 0)
    prev = jnp.where(t == 0, 0.0, pltpu.roll(cu, 1, axis=0))
    nxt = jnp.where(t == s - 1, 0.0, pltpu.roll(cu, s - 1, axis=0))
    y = b * (w[0:1] * prev + w[1:2] * cu + w[2:3] * nxt)
    return y * _rms_scale(y) * gain


def _attention_kernel(q_ref, k_ref, v_ref, qg_ref, kg_ref, cos_ref, sin_ref, g_ref,
                      cb_ref, cc_ref, cu_ref, cw_ref, cg_ref, *refs,
                      group, head_dim, unit_rows, ramp_rows, conv_group_dim, n_cast):
    cast_in, (o_ref, co_ref, *cast_out), (kt_ref, vx_ref) = (
        refs[:n_cast], refs[n_cast:2 * n_cast + 2], refs[2 * n_cast + 2:])
    seq_len = k_ref.shape[0]
    tq = q_ref.shape[0]
    i = pl.program_id(2)

    @pl.when(i == 0)
    def _():
        k = _norm_rope(k_ref[...].astype(F32), kg_ref[...], cos_ref[...], sin_ref[...])
        kt_ref[...] = k.astype(BF16).T
        vx_ref[:, :head_dim] = v_ref[...]
        vx_ref[:, head_dim:] = jnp.ones((seq_len, head_dim), BF16)

    q_scale = float(head_dim) ** -0.5 * LOG2_E
    for h in range(group):
        cols = slice(h * head_dim, (h + 1) * head_dim)
        bounds = list(range(0, tq + 1, unit_rows))
        if h == 0 and unit_rows > ramp_rows:
            bounds.insert(1, ramp_rows)
        if h == group - 1 and unit_rows > ramp_rows:
            bounds.insert(-1, tq - ramp_rows)
        for r, r_end in zip(bounds[:-1], bounds[1:]):
            rows = slice(r, r_end)
            pos = pl.ds(pl.multiple_of(i * tq + r, ramp_rows), r_end - r)
            q = _norm_rope(q_ref[rows, cols].astype(F32), qg_ref[...], cos_ref[pos, :], sin_ref[pos, :])
            s = jnp.dot((q * q_scale).astype(BF16), kt_ref[...], preferred_element_type=F32)
            m = jnp.max(s, axis=-1, keepdims=True)
            p = jnp.exp2(s - m).astype(BF16)
            ol = jnp.dot(p, vx_ref[...], preferred_element_type=F32)
            o = ol[:, :head_dim] / ol[:, head_dim:]
            o_ref[rows, cols] = (o * _rms_scale(o) * g_ref[:, cols]).astype(BF16)

    for c0 in range(0, cb_ref.shape[1], conv_group_dim):
        cc = slice(c0, c0 + conv_group_dim)
        y = _gated_conv(cb_ref[:, cc].astype(F32), cc_ref[:, cc].astype(F32), cu_ref[:, cc].astype(F32),
                        cw_ref[:, cc], cg_ref[:, cc])
        co_ref[:, cc] = y.astype(BF16)

    for w_in_ref, w_out_ref in zip(cast_in, cast_out):
        w_out_ref[...] = w_in_ref[...].astype(BF16)


def _attention(z, q_g, k_g, cos, sin, attn_g, conv_w, conv_g, cast_weights, *,
               n_batch, seq_len, attn_width, kv_width):
    m = z.shape[0]
    head_dim = q_g.shape[-1]
    conv_width = conv_g.shape[-1]
    n_kv = N_KV_HEADS
    group = attn_width // head_dim // n_kv
    assert kv_width == n_kv * head_dim
    tq = _largest_tile(seq_len, 1024, 16)
    q_blocks = seq_len // tq
    gw = group * head_dim
    k_col0 = attn_width // head_dim
    v_col0 = (attn_width + kv_width) // head_dim
    steps_per_batch = n_kv * q_blocks
    conv_cols = conv_width // steps_per_batch
    conv_group_dim = conv_width // CONV_GROUPS
    conv_col0 = attn_width + 2 * kv_width
    assert conv_width % steps_per_batch == 0 and conv_cols % conv_group_dim == 0 and conv_col0 % conv_cols == 0
    n_conv = conv_width // conv_cols
    kern = functools.partial(_attention_kernel, group=group, head_dim=head_dim,
                             unit_rows=_largest_tile(tq, 256, 16), ramp_rows=_largest_tile(tq, 64, 16),
                             conv_group_dim=conv_group_dim, n_cast=len(cast_weights))
    n_steps = n_batch * steps_per_batch
    step_map = lambda b, h, i: ((b * n_kv + h) * q_blocks + i, 0)
    cast_specs = []
    for w in cast_weights:
        assert w.shape[0] % (n_steps * BF16_SUBLANES) == 0, (w.shape, n_steps)
        cast_specs.append(pl.BlockSpec((w.shape[0] // n_steps, w.shape[1]), step_map))
    whole = lambda b, h, i: (0, 0)
    q_map = lambda b, h, i: (b * q_blocks + i, h)

    def conv_in(part):
        return pl.BlockSpec((seq_len, conv_cols),
                            lambda b, h, i: (b, conv_col0 // conv_cols + part * n_conv + h * q_blocks + i))

    conv_par = lambda b, h, i: (0, h * q_blocks + i)
    once = dict(pipeline_mode=pl.Buffered(1))
    return pl.pallas_call(
        kern,
        grid=(n_batch, n_kv, q_blocks),
        in_specs=[
            pl.BlockSpec((tq, gw), q_map),
            pl.BlockSpec((seq_len, head_dim), lambda b, h, i: (b, k_col0 + h)),
            pl.BlockSpec((seq_len, head_dim), lambda b, h, i: (b, v_col0 + h)),
            pl.BlockSpec((1, head_dim), whole),
            pl.BlockSpec((1, head_dim), whole),
            pl.BlockSpec((seq_len, head_dim), whole, **once),
            pl.BlockSpec((seq_len, head_dim), whole, **once),
            pl.BlockSpec((1, gw), lambda b, h, i: (0, h)),
            conv_in(0), conv_in(1), conv_in(2),
            pl.BlockSpec((conv_w.shape[0], conv_cols), conv_par),
            pl.BlockSpec((1, conv_cols), conv_par),
            *cast_specs,
        ],
        out_specs=[pl.BlockSpec((tq, gw), q_map),
                   pl.BlockSpec((seq_len, conv_cols), lambda b, h, i: (b, h * q_blocks + i)),
                   *cast_specs],
        out_shape=[jax.ShapeDtypeStruct((m, attn_width), BF16), jax.ShapeDtypeStruct((m, conv_width), BF16),
                   *(jax.ShapeDtypeStruct(w.shape, BF16) for w in cast_weights)],
        scratch_shapes=[pltpu.VMEM((head_dim, seq_len), BF16), pltpu.VMEM((seq_len, 2 * head_dim), BF16)],
        compiler_params=pltpu.CompilerParams(
            dimension_semantics=("arbitrary", "arbitrary", "arbitrary"),
            vmem_limit_bytes=VMEM_LIMIT_BYTES),
        name="attention",
    )(z, z, z, q_g, k_g, cos, sin, attn_g, z, z, z, conv_w, conv_g, *cast_weights)


def _out_proj_kernel(x_ref, a_ref, c_ref, wa_ref, wc_ref, g_ref, o_ref, h_ref, ss_ref):
    acc = jnp.dot(a_ref[...], wa_ref[...], preferred_element_type=F32)
    acc += jnp.dot(c_ref[...], wc_ref[...], preferred_element_type=F32)
    x1 = x_ref[...] + acc
    o_ref[...] = x1
    h_ref[...] = (x1 * g_ref[...]).astype(BF16)
    part = jnp.sum(x1 * x1, axis=-1, keepdims=True)

    @pl.when(pl.program_id(1) == 0)
    def _():
        ss_ref[...] = part

    @pl.when(pl.program_id(1) > 0)
    def _():
        ss_ref[...] += part


def _out_proj(x2, a, c, w_out, mlp_g):
    m, d = x2.shape
    ka = a.shape[1]
    kc = c.shape[1]
    assert ka == kc and w_out.shape[0] == ka + kc
    tm = _largest_tile(m, 1024, 16)
    tn = _largest_tile(d, 1024, 128)
    return pl.pallas_call(
        _out_proj_kernel,
        grid=(m // tm, d // tn),
        in_specs=[
            pl.BlockSpec((tm, tn), lambda i, j: (i, j)),
            pl.BlockSpec((tm, ka), lambda i, j: (i, 0)),
            pl.BlockSpec((tm, kc), lambda i, j: (i, 0)),
            pl.BlockSpec((ka, tn), lambda i, j: (0, j)),
            pl.BlockSpec((kc, tn), lambda i, j: (1, j)),
            pl.BlockSpec((1, tn), lambda i, j: (0, j)),
        ],
        out_specs=[
            pl.BlockSpec((tm, tn), lambda i, j: (i, j)),
            pl.BlockSpec((tm, tn), lambda i, j: (i, j)),
            pl.BlockSpec((tm, 1), lambda i, j: (i, 0)),
        ],
        out_shape=[jax.ShapeDtypeStruct((m, d), F32), jax.ShapeDtypeStruct((m, d), BF16),
                   jax.ShapeDtypeStruct((m, 1), F32)],
        compiler_params=pltpu.CompilerParams(
            dimension_semantics=("parallel", "arbitrary"), vmem_limit_bytes=VMEM_LIMIT_BYTES),
        name="out_proj",
    )(x2, a, c, w_out, w_out, mlp_g)


def _mlp_kernel(x_ref, h_ref, ss_ref, wu_ref, wd_ref, fg_ref, o_ref, *, slab_cols):
    f = pl.program_id(1)
    last = pl.num_programs(1) - 1

    def act():
        u = jnp.dot(h_ref[...], wu_ref[...], preferred_element_type=F32)
        return jnp.square(jnp.maximum(u, 0.0)).astype(BF16)

    def down(a, cols):
        return jnp.dot(a, wd_ref[:, cols], preferred_element_type=F32)

    d = o_ref.shape[1]
    slabs = [slice(n, n + slab_cols) for n in range(0, d, slab_cols)]

    @pl.when(f == 0)
    def _():
        a = act()
        for cols in slabs:
            o_ref[:, cols] = down(a, cols)

    @pl.when((f > 0) & (f < last))
    def _():
        a = act()
        for cols in slabs:
            o_ref[:, cols] += down(a, cols)

    @pl.when(f == last)
    def _():
        a = act()
        r2 = 1.0 / (ss_ref[...] * (1.0 / d) + EPS)
        for cols in slabs:
            o_ref[:, cols] = x_ref[:, cols] + r2 * (o_ref[:, cols] + down(a, cols))
        y = o_ref[...]
        o_ref[...] = y * _rms_scale(y) * fg_ref[...]


def _mlp(x1, h, ss, w_up, w_down, final_g):
    m, d = x1.shape
    d_ff = w_up.shape[1]
    tm = _largest_tile(m, 512, 16)
    tf = _largest_tile(d_ff, 512, 128)
    assert d_ff // tf >= 2
    return pl.pallas_call(
        functools.partial(_mlp_kernel, slab_cols=_largest_tile(d, 1024, 128)),
        grid=(m // tm, d_ff // tf),
        in_specs=[
            pl.BlockSpec((tm, d), lambda i, f: (i, 0)),
            pl.BlockSpec((tm, d), lambda i, f: (i, 0)),
            pl.BlockSpec((tm, 1), lambda i, f: (i, 0)),
            pl.BlockSpec((d, tf), lambda i, f: (0, f)),
            pl.BlockSpec((tf, d), lambda i, f: (f, 0)),
            pl.BlockSpec((1, d), lambda i, f: (0, 0)),
        ],
        out_specs=pl.BlockSpec((tm, d), lambda i, f: (i, 0)),
        out_shape=jax.ShapeDtypeStruct((m, d), F32),
        compiler_params=pltpu.CompilerParams(
            dimension_semantics=("parallel", "arbitrary"), vmem_limit_bytes=VMEM_LIMIT_BYTES),
        name="mlp",
    )(x1, h, ss, w_up, w_down, final_g)


def _trunk(x, w_in, q_g, k_g, conv_w, attn_g, conv_g, mix_g, mlp_g, final_g, cos, sin, late_weights, cast_late):
    n_batch, seq_len, d = x.shape
    attn_width = attn_g.shape[-1]
    conv_width = conv_g.shape[-1]
    kv_width = (w_in.shape[1] - attn_width - 3 * conv_width) // 2
    x2 = x.reshape(n_batch * seq_len, d)
    w_out, w_up, w_down = late_weights
    z, *cast = _in_proj(x2, mix_g, w_in, (w_out, w_down) if cast_late else ())
    if cast_late:
        w_out, w_down = cast
    a, c, *cast = _attention(z, q_g, k_g, cos, sin, attn_g, conv_w, conv_g, (w_up,) if cast_late else (),
                             n_batch=n_batch, seq_len=seq_len, attn_width=attn_width, kv_width=kv_width)
    if cast_late:
        (w_up,) = cast
    x1, h, ss = _out_proj(x2, a, c, w_out, mlp_g)
    y = _mlp(x1, h, ss, w_up, w_down, final_g)
    return y.reshape(n_batch, seq_len, d), (w_out, w_up, w_down)


def kernel(x_prompt, x_sample, w_in, q_norm, k_norm, conv_w, attn_grp_norm, conv_grp_norm, w_out,
           mix_norm, mlp_norm, w_up, w_down, final_norm):
    assert w_in.shape[0] == 1, "single-layer trunk"
    head_dim = q_norm.shape[-1]
    params = (w_in[0].astype(BF16), q_norm, k_norm, conv_w[0], attn_grp_norm, conv_grp_norm,
              mix_norm, mlp_norm, final_norm.reshape(1, -1))
    late_weights, cast_late = (w_out[0], w_up[0], w_down[0]), True
    outs = []
    for x in (x_prompt, x_sample):
        cos, sin = _rope_tables(x.shape[1], head_dim)
        y, late_weights = _trunk(x, *params, cos, sin, late_weights, cast_late)
        cast_late = False
        outs.append(y)
    return tuple(outs)
```

```python
import functools

import jax
import jax.numpy as jnp
from jax import lax
from jax.experimental import pallas as pl
from jax.experimental.pallas import tpu as pltpu

EPS = 1e-6
GRID_W = 64
ROPE_THETA = 10000.0
N_KV_HEADS = 4
CONV_GROUPS = 16
LOG2_E = 1.4426950408889634

V7X_VMEM_BYTES = 64 * 1024 * 1024
VMEM_LIMIT_BYTES = V7X_VMEM_BYTES - 2 * 1024 * 1024

F32 = jnp.float32
BF16 = jnp.bfloat16
BF16_SUBLANES = 16


def _largest_tile(dim, target, quantum):
    t = min(target, dim)
    t -= t % quantum
    while t > quantum and dim % t:
        t -= quantum
    assert t >= quantum and dim % t == 0, (dim, target, quantum)
    return t


def _rope_tables(seq_len, head_dim):
    axis_dim = head_dim // 2
    n_rows = seq_len // GRID_W
    row = jnp.repeat(jnp.arange(n_rows, dtype=F32), GRID_W)
    col = jnp.tile(jnp.arange(GRID_W, dtype=F32), n_rows)
    freqs = ROPE_THETA ** (-jnp.arange(0, axis_dim, 2, dtype=F32) / axis_dim)
    ang_r = row[:, None] * freqs
    ang_c = col[:, None] * freqs
    cos = jnp.concatenate([jnp.cos(ang_r), jnp.cos(ang_r), jnp.cos(ang_c), jnp.cos(ang_c)], axis=-1)
    sin = jnp.concatenate([-jnp.sin(ang_r), jnp.sin(ang_r), -jnp.sin(ang_c), jnp.sin(ang_c)], axis=-1)
    return cos, sin


def _rms_scale(x):
    return lax.rsqrt(jnp.mean(x * x, axis=-1, keepdims=True) + EPS)


def _in_proj_kernel(xs_ref, g_ref, w_ref, *refs, n_cast, n_slices):
    cast_in, (z_ref, *cast_out), (h_ref, r_ref) = refs[:n_cast], refs[n_cast:2 * n_cast + 1], refs[2 * n_cast + 1:]
    i = pl.program_id(0)
    j = pl.program_id(1)
    slice_rows = xs_ref.shape[0]

    def side_jobs():
        sl = jnp.where(i == pl.num_programs(0) - 1, n_slices - 1, jnp.clip(j - 1, 0, n_slices - 1))
        rows = pl.ds(pl.multiple_of(sl * slice_rows, slice_rows), slice_rows)
        xf = xs_ref[...]
        h_ref[i % 2, rows, :] = (xf * g_ref[...]).astype(BF16)
        r_ref[i % 2, rows, :] = _rms_scale(xf)
        for w_in_ref, w_out_ref in zip(cast_in, cast_out):
            w_out_ref[...] = w_in_ref[...].astype(BF16)

    @pl.when(i == 0)
    def _():
        z_ref[...] = jnp.zeros(z_ref.shape, BF16)
        side_jobs()

    @pl.when(i > 0)
    def _():
        slot = (i - 1) % 2
        acc = jnp.dot(h_ref[slot], w_ref[...], preferred_element_type=F32)
        z_ref[...] = (acc * r_ref[slot]).astype(BF16)
        side_jobs()


def _in_proj(x2, mix_g, w_in, cast_weights):
    m, d = x2.shape
    n = w_in.shape[1]
    tm = _largest_tile(m, 512, 16)
    tn = _largest_tile(n, 1024, 128)
    n_i, n_j = m // tm, n // tn
    assert n_j >= 2
    n_slices = 1 << ((n_j - 1).bit_length() - 1)
    assert tm % (n_slices * BF16_SUBLANES) == 0
    slice_of = lambda i, j: jnp.where(i == n_i, n_slices - 1, jnp.clip(j - 1, 0, n_slices - 1))
    tile_of = lambda i: jnp.minimum(i, n_i - 1)
    done_of = lambda i: jnp.where(i == 0, n_i, i - 1)
    cast_map = lambda i, j: (tile_of(i) * n_slices + slice_of(i, j), 0)
    cast_specs = []
    for w in cast_weights:
        assert w.shape[0] % (n_i * n_slices * BF16_SUBLANES) == 0, (w.shape, n_i, n_slices)
        cast_specs.append(pl.BlockSpec((w.shape[0] // (n_i * n_slices), w.shape[1]), cast_map))
    return pl.pallas_call(
        functools.partial(_in_proj_kernel, n_cast=len(cast_weights), n_slices=n_slices),
        grid=(n_i + 1, n_j),
        in_specs=[
            pl.BlockSpec((tm // n_slices, d), lambda i, j: (tile_of(i) * n_slices + slice_of(i, j), 0)),
            pl.BlockSpec((1, d), lambda i, j: (0, 0)),
            pl.BlockSpec((d, tn), lambda i, j: (0, jnp.where(i == 0, 0, j))),
            *cast_specs,
        ],
        out_specs=[pl.BlockSpec((tm, tn), lambda i, j: (done_of(i), jnp.where(i == 0, 0, j))), *cast_specs],
        out_shape=[jax.ShapeDtypeStruct((m + tm, n), BF16),
                   *(jax.ShapeDtypeStruct(w.shape, BF16) for w in cast_weights)],
        scratch_shapes=[pltpu.VMEM((2, tm, d), BF16), pltpu.VMEM((2, tm, 1), F32)],
        compiler_params=pltpu.CompilerParams(
            dimension_semantics=("arbitrary", "arbitrary"), vmem_limit_bytes=VMEM_LIMIT_BYTES),
        name="in_proj",
    )(x2, mix_g, w_in, *cast_weights)


def _norm_rope(a, gain, cos, sin):
    quarter = a.shape[1] // 4
    a = a * _rms_scale(a) * gain
    lane = lax.broadcasted_iota(jnp.int32, a.shape, 1)
    up = pltpu.roll(a, a.shape[1] - quarter, axis=1)
    down = pltpu.roll(a, quarter, axis=1)
    return a * cos + jnp.where((lane & quarter) == 0, up, down) * sin


def _gated_conv(b, c, u, w, gain):
    s = b.shape[0]
    cu = c * u
    t = lax.broadcasted_iota(jnp.int32, cu.shape, 0)
    prev = jnp.where(t == 0, 0.0, pltpu.roll(cu, 1, axis=0))
    nxt = jnp.where(t == s - 1, 0.0, pltpu.roll(cu, s - 1, axis=0))
    y = b * (w[0:1] * prev + w[1:2] * cu + w[2:3] * nxt)
    return y * _rms_scale(y) * gain


def _attention_kernel(q_ref, k_ref, v_ref, qg_ref, kg_ref, cos_ref, sin_ref, g_ref,
                      cb_ref, cc_ref, cu_ref, cw_ref, cg_ref, *refs,
                      group, head_dim, unit_rows, ramp_rows, conv_group_dim, n_cast):
    cast_in, (o_ref, co_ref, *cast_out), (kt_ref, vx_ref) = (
        refs[:n_cast], refs[n_cast:2 * n_cast + 2], refs[2 * n_cast + 2:])
    seq_len = k_ref.shape[0]
    tq = q_ref.shape[0]
    i = pl.program_id(2)

    @pl.when(i == 0)
    def _():
        k = _norm_rope(k_ref[...].astype(F32), kg_ref[...], cos_ref[...], sin_ref[...])
        kt_ref[...] = k.astype(BF16).T
        vx_ref[:, :head_dim] = v_ref[...]
        vx_ref[:, head_dim:] = jnp.ones((seq_len, head_dim), BF16)

    q_scale = float(head_dim) ** -0.5 * LOG2_E
    for h in range(group):
        cols = slice(h * head_dim, (h + 1) * head_dim)
        bounds = list(range(0, tq + 1, unit_rows))
        if h == 0 and unit_rows > ramp_rows:
            bounds.insert(1, ramp_rows)
        if h == group - 1 and unit_rows > ramp_rows:
            bounds.insert(-1, tq - ramp_rows)
        for r, r_end in zip(bounds[:-1], bounds[1:]):
            rows = slice(r, r_end)
            pos = pl.ds(pl.multiple_of(i * tq + r, ramp_rows), r_end - r)
            q = _norm_rope(q_ref[rows, cols].astype(F32), qg_ref[...], cos_ref[pos, :], sin_ref[pos, :])
            s = jnp.dot((q * q_scale).astype(BF16), kt_ref[...], preferred_element_type=F32)
            m = jnp.max(s, axis=-1, keepdims=True)
            p = jnp.exp2(s - m).astype(BF16)
            ol = jnp.dot(p, vx_ref[...], preferred_element_type=F32)
            o = ol[:, :head_dim] / ol[:, head_dim:]
            o_ref[rows, cols] = (o * _rms_scale(o) * g_ref[:, cols]).astype(BF16)

    for c0 in range(0, cb_ref.shape[1], conv_group_dim):
        cc = slice(c0, c0 + conv_group_dim)
        y = _gated_conv(cb_ref[:, cc].astype(F32), cc_ref[:, cc].astype(F32), cu_ref[:, cc].astype(F32),
                        cw_ref[:, cc], cg_ref[:, cc])
        co_ref[:, cc] = y.astype(BF16)

    for w_in_ref, w_out_ref in zip(cast_in, cast_out):
        w_out_ref[...] = w_in_ref[...].astype(BF16)


def _attention(z, q_g, k_g, cos, sin, attn_g, conv_w, conv_g, cast_weights, *,
               n_batch, seq_len, attn_width, kv_width):
    m = n_batch * seq_len
    head_dim = q_g.shape[-1]
    conv_width = conv_g.shape[-1]
    n_kv = N_KV_HEADS
    group = attn_width // head_dim // n_kv
    assert kv_width == n_kv * head_dim
    tq = _largest_tile(seq_len, 1024, 16)
    q_blocks = seq_len // tq
    gw = group * head_dim
    k_col0 = attn_width // head_dim
    v_col0 = (attn_width + kv_width) // head_dim
    steps_per_batch = n_kv * q_blocks
    conv_cols = conv_width // steps_per_batch
    conv_group_dim = conv_width // CONV_GROUPS
    conv_col0 = attn_width + 2 * kv_width
    assert conv_width % steps_per_batch == 0 and conv_cols % conv_group_dim == 0 and conv_col0 % conv_cols == 0
    n_conv = conv_width // conv_cols
    kern = functools.partial(_attention_kernel, group=group, head_dim=head_dim,
                             unit_rows=_largest_tile(tq, 256, 16), ramp_rows=_largest_tile(tq, 64, 16),
                             conv_group_dim=conv_group_dim, n_cast=len(cast_weights))
    n_steps = n_batch * steps_per_batch
    step_map = lambda b, h, i: ((b * n_kv + h) * q_blocks + i, 0)
    cast_specs = []
    for w in cast_weights:
        assert w.shape[0] % (n_steps * BF16_SUBLANES) == 0, (w.shape, n_steps)
        cast_specs.append(pl.BlockSpec((w.shape[0] // n_steps, w.shape[1]), step_map))
    whole = lambda b, h, i: (0, 0)
    q_map = lambda b, h, i: (b * q_blocks + i, h)

    def conv_in(part):
        return pl.BlockSpec((seq_len, conv_cols),
                            lambda b, h, i: (b, conv_col0 // conv_cols + part * n_conv + h * q_blocks + i))

    conv_par = lambda b, h, i: (0, h * q_blocks + i)
    once = dict(pipeline_mode=pl.Buffered(1))
    return pl.pallas_call(
        kern,
        grid=(n_batch, n_kv, q_blocks),
        in_specs=[
            pl.BlockSpec((tq, gw), q_map),
            pl.BlockSpec((seq_len, head_dim), lambda b, h, i: (b, k_col0 + h)),
            pl.BlockSpec((seq_len, head_dim), lambda b, h, i: (b, v_col0 + h)),
            pl.BlockSpec((1, head_dim), whole),
            pl.BlockSpec((1, head_dim), whole),
            pl.BlockSpec((seq_len, head_dim), whole, **once),
            pl.BlockSpec((seq_len, head_dim), whole, **once),
            pl.BlockSpec((1, gw), lambda b, h, i: (0, h)),
            conv_in(0), conv_in(1), conv_in(2),
            pl.BlockSpec((conv_w.shape[0], conv_cols), conv_par),
            pl.BlockSpec((1, conv_cols), conv_par),
            *cast_specs,
        ],
        out_specs=[pl.BlockSpec((tq, gw), q_map),
                   pl.BlockSpec((seq_len, conv_cols), lambda b, h, i: (b, h * q_blocks + i)),
                   *cast_specs],
        out_shape=[jax.ShapeDtypeStruct((m, attn_width), BF16), jax.ShapeDtypeStruct((m, conv_width), BF16),
                   *(jax.ShapeDtypeStruct(w.shape, BF16) for w in cast_weights)],
        scratch_shapes=[pltpu.VMEM((head_dim, seq_len), BF16), pltpu.VMEM((seq_len, 2 * head_dim), BF16)],
        compiler_params=pltpu.CompilerParams(
            dimension_semantics=("arbitrary", "arbitrary", "arbitrary"),
            vmem_limit_bytes=VMEM_LIMIT_BYTES),
        name="attention",
    )(z, z, z, q_g, k_g, cos, sin, attn_g, z, z, z, conv_w, conv_g, *cast_weights)


def _out_proj_kernel(x_ref, a_ref, c_ref, wa_ref, wc_ref, g_ref, o_ref, h_ref, ss_ref):
    acc = jnp.dot(a_ref[...], wa_ref[...], preferred_element_type=F32)
    acc += jnp.dot(c_ref[...], wc_ref[...], preferred_element_type=F32)
    x1 = x_ref[...] + acc
    o_ref[...] = x1
    h_ref[...] = (x1 * g_ref[...]).astype(BF16)
    part = jnp.sum(x1 * x1, axis=-1, keepdims=True)

    @pl.when(pl.program_id(1) == 0)
    def _():
        ss_ref[...] = part

    @pl.when(pl.program_id(1) > 0)
    def _():
        ss_ref[...] += part


def _out_proj(x2, a, c, w_out, mlp_g):
    m, d = x2.shape
    ka = a.shape[1]
    kc = c.shape[1]
    assert ka == kc and w_out.shape[0] == ka + kc
    tm = _largest_tile(m, 1024, 16)
    tn = _largest_tile(d, 1024, 128)
    return pl.pallas_call(
        _out_proj_kernel,
        grid=(m // tm, d // tn),
        in_specs=[
            pl.BlockSpec((tm, tn), lambda i, j: (i, j)),
            pl.BlockSpec((tm, ka), lambda i, j: (i, 0)),
            pl.BlockSpec((tm, kc), lambda i, j: (i, 0)),
            pl.BlockSpec((ka, tn), lambda i, j: (0, j)),
            pl.BlockSpec((kc, tn), lambda i, j: (1, j)),
            pl.BlockSpec((1, tn), lambda i, j: (0, j)),
        ],
        out_specs=[
            pl.BlockSpec((tm, tn), lambda i, j: (i, j)),
            pl.BlockSpec((tm, tn), lambda i, j: (i, j)),
            pl.BlockSpec((tm, 1), lambda i, j: (i, 0)),
        ],
        out_shape=[jax.ShapeDtypeStruct((m, d), F32), jax.ShapeDtypeStruct((m, d), BF16),
                   jax.ShapeDtypeStruct((m, 1), F32)],
        compiler_params=pltpu.CompilerParams(
            dimension_semantics=("parallel", "arbitrary"), vmem_limit_bytes=VMEM_LIMIT_BYTES),
        name="out_proj",
    )(x2, a, c, w_out, w_out, mlp_g)


def _mlp_kernel(x_ref, h_ref, ss_ref, wu_ref, wd_ref, fg_ref, o_ref, *, slab_cols):
    f = pl.program_id(1)
    last = pl.num_programs(1) - 1

    def act():
        u = jnp.dot(h_ref[...], wu_ref[...], preferred_element_type=F32)
        return jnp.square(jnp.maximum(u, 0.0)).astype(BF16)

    def down(a, cols):
        return jnp.dot(a, wd_ref[:, cols], preferred_element_type=F32)

    d = o_ref.shape[1]
    slabs = [slice(n, n + slab_cols) for n in range(0, d, slab_cols)]

    @pl.when(f == 0)
    def _():
        a = act()
        for cols in slabs:
            o_ref[:, cols] = down(a, cols)

    @pl.when((f > 0) & (f < last))
    def _():
        a = act()
        for cols in slabs:
            o_ref[:, cols] += down(a, cols)

    @pl.when(f == last)
    def _():
        a = act()
        r2 = 1.0 / (ss_ref[...] * (1.0 / d) + EPS)
        for cols in slabs:
            o_ref[:, cols] = x_ref[:, cols] + r2 * (o_ref[:, cols] + down(a, cols))
        y = o_ref[...]
        o_ref[...] = y * _rms_scale(y) * fg_ref[...]


def _mlp(x1, h, ss, w_up, w_down, final_g):
    m, d = x1.shape
    d_ff = w_up.shape[1]
    tm = _largest_tile(m, 512, 16)
    tf = _largest_tile(d_ff, 512, 128)
    assert d_ff // tf >= 2
    return pl.pallas_call(
        functools.partial(_mlp_kernel, slab_cols=_largest_tile(d, 1024, 128)),
        grid=(m // tm, d_ff // tf),
        in_specs=[
            pl.BlockSpec((tm, d), lambda i, f: (i, 0)),
            pl.BlockSpec((tm, d), lambda i, f: (i, 0)),
            pl.BlockSpec((tm, 1), lambda i, f: (i, 0)),
            pl.BlockSpec((d, tf), lambda i, f: (0, f)),
            pl.BlockSpec((tf, d), lambda i, f: (f, 0)),
            pl.BlockSpec((1, d), lambda i, f: (0, 0)),
        ],
        out_specs=pl.BlockSpec((tm, d), lambda i, f: (i, 0)),
        out_shape=jax.ShapeDtypeStruct((m, d), F32),
        compiler_params=pltpu.CompilerParams(
            dimension_semantics=("parallel", "arbitrary"), vmem_limit_bytes=VMEM_LIMIT_BYTES),
        name="mlp",
    )(x1, h, ss, w_up, w_down, final_g)


def _trunk(x, w_in, q_g, k_g, conv_w, attn_g, conv_g, mix_g, mlp_g, final_g, cos, sin, late_weights, cast_late):
    n_batch, seq_len, d = x.shape
    attn_width = attn_g.shape[-1]
    conv_width = conv_g.shape[-1]
    kv_width = (w_in.shape[1] - attn_width - 3 * conv_width) // 2
    x2 = x.reshape(n_batch * seq_len, d)
    w_out, w_up, w_down = late_weights
    z, *cast = _in_proj(x2, mix_g, w_in, (w_out, w_down) if cast_late else ())
    if cast_late:
        w_out, w_down = cast
    a, c, *cast = _attention(z, q_g, k_g, cos, sin, attn_g, conv_w, conv_g, (w_up,) if cast_late else (),
                             n_batch=n_batch, seq_len=seq_len, attn_width=attn_width, kv_width=kv_width)
    if cast_late:
        (w_up,) = cast
    x1, h, ss = _out_proj(x2, a, c, w_out, mlp_g)
    y = _mlp(x1, h, ss, w_up, w_down, final_g)
    return y.reshape(n_batch, seq_len, d), (w_out, w_up, w_down)


def kernel(x_prompt, x_sample, w_in, q_norm, k_norm, conv_w, attn_grp_norm, conv_grp_norm, w_out,
           mix_norm, mlp_norm, w_up, w_down, final_norm):
    assert w_in.shape[0] == 1, "single-layer trunk"
    head_dim = q_norm.shape[-1]
    params = (w_in[0].astype(BF16), q_norm, k_norm, conv_w[0], attn_grp_norm, conv_grp_norm,
              mix_norm, mlp_norm, final_norm.reshape(1, -1))
    late_weights, cast_late = (w_out[0], w_up[0], w_down[0]), True
    outs = []
    for x in (x_prompt, x_sample):
        cos, sin = _rope_tables(x.shape[1], head_dim)
        y, late_weights = _trunk(x, *params, cos, sin, late_weights, cast_late)
        cast_late = False
        outs.append(y)
    return tuple(outs)
```

```python
import functools

import jax
import jax.numpy as jnp
from jax import lax
from jax.experimental import pallas as pl
from jax.experimental.pallas import tpu as pltpu

EPS = 1e-6
GRID_W = 64
ROPE_THETA = 10000.0
N_KV_HEADS = 4
CONV_GROUPS = 16
LOG2_E = 1.4426950408889634

V7X_VMEM_BYTES = 64 * 1024 * 1024
VMEM_LIMIT_BYTES = V7X_VMEM_BYTES - 2 * 1024 * 1024

F32 = jnp.float32
BF16 = jnp.bfloat16
BF16_SUBLANES = 16


def _largest_tile(dim, target, quantum):
    t = min(target, dim)
    t -= t % quantum
    while t > quantum and dim % t:
        t -= quantum
    assert t >= quantum and dim % t == 0, (dim, target, quantum)
    return t


def _rope_tables(seq_len, head_dim):
    axis_dim = head_dim // 2
    n_rows = seq_len // GRID_W
    row = jnp.repeat(jnp.arange(n_rows, dtype=F32), GRID_W)
    col = jnp.tile(jnp.arange(GRID_W, dtype=F32), n_rows)
    freqs = ROPE_THETA ** (-jnp.arange(0, axis_dim, 2, dtype=F32) / axis_dim)
    ang_r = row[:, None] * freqs
    ang_c = col[:, None] * freqs
    cos = jnp.concatenate([jnp.cos(ang_r), jnp.cos(ang_r), jnp.cos(ang_c), jnp.cos(ang_c)], axis=-1)
    sin = jnp.concatenate([-jnp.sin(ang_r), jnp.sin(ang_r), -jnp.sin(ang_c), jnp.sin(ang_c)], axis=-1)
    return cos, sin


def _rms_scale(x):
    return lax.rsqrt(jnp.mean(x * x, axis=-1, keepdims=True) + EPS)


def _in_proj_kernel(xs_ref, g_ref, w_ref, *refs, n_cast, n_slices):
    cast_in, (z_ref, *cast_out), (h_ref, r_ref) = refs[:n_cast], refs[n_cast:2 * n_cast + 1], refs[2 * n_cast + 1:]
    i = pl.program_id(0)
    j = pl.program_id(1)
    slice_rows = xs_ref.shape[0]

    def side_jobs():
        sl = jnp.where(i == pl.num_programs(0) - 1, n_slices - 1, jnp.clip(j - 1, 0, n_slices - 1))
        rows = pl.ds(pl.multiple_of(sl * slice_rows, slice_rows), slice_rows)
        xf = xs_ref[...]
        h_ref[i % 2, rows, :] = (xf * g_ref[...]).astype(BF16)
        r_ref[i % 2, rows, :] = _rms_scale(xf)
        for w_in_ref, w_out_ref in zip(cast_in, cast_out):
            w_out_ref[...] = w_in_ref[...].astype(BF16)

    @pl.when(i == 0)
    def _():
        z_ref[...] = jnp.zeros(z_ref.shape, BF16)
        side_jobs()

    @pl.when(i > 0)
    def _():
        slot = (i - 1) % 2
        acc = jnp.dot(h_ref[slot], w_ref[...], preferred_element_type=F32)
        z_ref[...] = (acc * r_ref[slot]).astype(BF16)
        side_jobs()


def _in_proj(x2, mix_g, w_in, cast_weights):
    m, d = x2.shape
    n = w_in.shape[1]
    tm = _largest_tile(m, 512, 16)
    tn = _largest_tile(n, 1024, 128)
    n_i, n_j = m // tm, n // tn
    assert n_j >= 2
    n_slices = 1 << ((n_j - 1).bit_length() - 1)
    assert tm % (n_slices * BF16_SUBLANES) == 0
    slice_of = lambda i, j: jnp.where(i == n_i, n_slices - 1, jnp.clip(j - 1, 0, n_slices - 1))
    tile_of = lambda i: jnp.minimum(i, n_i - 1)
    done_of = lambda i: jnp.where(i == 0, n_i, i - 1)
    cast_map = lambda i, j: (tile_of(i) * n_slices + slice_of(i, j), 0)
    cast_specs = []
    for w in cast_weights:
        assert w.shape[0] % (n_i * n_slices * BF16_SUBLANES) == 0, (w.shape, n_i, n_slices)
        cast_specs.append(pl.BlockSpec((w.shape[0] // (n_i * n_slices), w.shape[1]), cast_map))
    return pl.pallas_call(
        functools.partial(_in_proj_kernel, n_cast=len(cast_weights), n_slices=n_slices),
        grid=(n_i + 1, n_j),
        in_specs=[
            pl.BlockSpec((tm // n_slices, d), lambda i, j: (tile_of(i) * n_slices + slice_of(i, j), 0)),
            pl.BlockSpec((1, d), lambda i, j: (0, 0)),
            pl.BlockSpec((d, tn), lambda i, j: (0, jnp.where(i == 0, 0, j))),
            *cast_specs,
        ],
        out_specs=[pl.BlockSpec((tm, tn), lambda i, j: (done_of(i), j)), *cast_specs],
        out_shape=[jax.ShapeDtypeStruct((m + tm, n), BF16),
                   *(jax.ShapeDtypeStruct(w.shape, BF16) for w in cast_weights)],
        scratch_shapes=[pltpu.VMEM((2, tm, d), BF16), pltpu.VMEM((2, tm, 1), F32)],
        compiler_params=pltpu.CompilerParams(
            dimension_semantics=("arbitrary", "arbitrary"), vmem_limit_bytes=VMEM_LIMIT_BYTES),
        name="in_proj",
    )(x2, mix_g, w_in, *cast_weights)


def _norm_rope(a, gain, cos, sin):
    quarter = a.shape[1] // 4
    a = a * _rms_scale(a) * gain
    lane = lax.broadcasted_iota(jnp.int32, a.shape, 1)
    up = pltpu.roll(a, a.shape[1] - quarter, axis=1)
    down = pltpu.roll(a, quarter, axis=1)
    return a * cos + jnp.where((lane & quarter) == 0, up, down) * sin


def _gated_conv(b, c, u, w, gain):
    s = b.shape[0]
    cu = c * u
    t = lax.broadcasted_iota(jnp.int32, cu.shape, 0)
    prev = jnp.where(t == 0, 0.0, pltpu.roll(cu, 1, axis=0))
    nxt = jnp.where(t == s - 1, 0.0, pltpu.roll(cu, s - 1, axis=0))
    y = b * (w[0:1] * prev + w[1:2] * cu + w[2:3] * nxt)
    return y * _rms_scale(y) * gain


def _attention_kernel(q_ref, k_ref, v_ref, qg_ref, kg_ref, cos_ref, sin_ref, g_ref,
                      cb_ref, cc_ref, cu_ref, cw_ref, cg_ref, *refs,
                      group, head_dim, unit_rows, ramp_rows, conv_group_dim, n_cast):
    cast_in, (o_ref, co_ref, *cast_out), (kt_ref, vx_ref) = (
        refs[:n_cast], refs[n_cast:2 * n_cast + 2], refs[2 * n_cast + 2:])
    seq_len = k_ref.shape[0]
    tq = q_ref.shape[0]
    i = pl.program_id(2)

    @pl.when(i == 0)
    def _():
        k = _norm_rope(k_ref[...].astype(F32), kg_ref[...], cos_ref[...], sin_ref[...])
        kt_ref[...] = k.astype(BF16).T
        vx_ref[:, :head_dim] = v_ref[...]
        vx_ref[:, head_dim:] = jnp.ones((seq_len, head_dim), BF16)

    q_scale = float(head_dim) ** -0.5 * LOG2_E
    for h in range(group):
        cols = slice(h * head_dim, (h + 1) * head_dim)
        bounds = list(range(0, tq + 1, unit_rows))
        if h == 0 and unit_rows > ramp_rows:
            bounds.insert(1, ramp_rows)
        if h == group - 1 and unit_rows > ramp_rows:
            bounds.insert(-1, tq - ramp_rows)
        for r, r_end in zip(bounds[:-1], bounds[1:]):
            rows = slice(r, r_end)
            pos = pl.ds(pl.multiple_of(i * tq + r, ramp_rows), r_end - r)
            q = _norm_rope(q_ref[rows, cols].astype(F32), qg_ref[...], cos_ref[pos, :], sin_ref[pos, :])
            s = jnp.dot((q * q_scale).astype(BF16), kt_ref[...], preferred_element_type=F32)
            m = jnp.max(s, axis=-1, keepdims=True)
            p = jnp.exp2(s - m).astype(BF16)
            ol = jnp.dot(p, vx_ref[...], preferred_element_type=F32)
            o = ol[:, :head_dim] / ol[:, head_dim:]
            o_ref[rows, cols] = (o * _rms_scale(o) * g_ref[:, cols]).astype(BF16)

    for c0 in range(0, cb_ref.shape[1], conv_group_dim):
        cc = slice(c0, c0 + conv_group_dim)
        y = _gated_conv(cb_ref[:, cc].astype(F32), cc_ref[:, cc].astype(F32), cu_ref[:, cc].astype(F32),
                        cw_ref[:, cc], cg_ref[:, cc])
        co_ref[:, cc] = y.astype(BF16)

    for w_in_ref, w_out_ref in zip(cast_in, cast_out):
        w_out_ref[...] = w_in_ref[...].astype(BF16)


def _attention(z, q_g, k_g, cos, sin, attn_g, conv_w, conv_g, cast_weights, *,
               n_batch, seq_len, attn_width, kv_width):
    m = n_batch * seq_len
    head_dim = q_g.shape[-1]
    conv_width = conv_g.shape[-1]
    n_kv = N_KV_HEADS
    group = attn_width // head_dim // n_kv
    assert kv_width == n_kv * head_dim
    tq = _largest_tile(seq_len, 1024, 16)
    q_blocks = seq_len // tq
    gw = group * head_dim
    k_col0 = attn_width // head_dim
    v_col0 = (attn_width + kv_width) // head_dim
    steps_per_batch = n_kv * q_blocks
    conv_cols = conv_width // steps_per_batch
    conv_group_dim = conv_width // CONV_GROUPS
    conv_col0 = attn_width + 2 * kv_width
    assert conv_width % steps_per_batch == 0 and conv_cols % conv_group_dim == 0 and conv_col0 % conv_cols == 0
    n_conv = conv_width // conv_cols
    kern = functools.partial(_attention_kernel, group=group, head_dim=head_dim,
                             unit_rows=_largest_tile(tq, 256, 16), ramp_rows=_largest_tile(tq, 64, 16),
                             conv_group_dim=conv_group_dim, n_cast=len(cast_weights))
    n_steps = n_batch * steps_per_batch
    step_map = lambda b, h, i: ((b * n_kv + h) * q_blocks + i, 0)
    cast_specs = []
    for w in cast_weights:
        assert w.shape[0] % (n_steps * BF16_SUBLANES) == 0, (w.shape, n_steps)
        cast_specs.append(pl.BlockSpec((w.shape[0] // n_steps, w.shape[1]), step_map))
    whole = lambda b, h, i: (0, 0)
    q_map = lambda b, h, i: (b * q_blocks + i, h)

    def conv_in(part):
        return pl.BlockSpec((seq_len, conv_cols),
                            lambda b, h, i: (b, conv_col0 // conv_cols + part * n_conv + h * q_blocks + i))

    conv_par = lambda b, h, i: (0, h * q_blocks + i)
    once = dict(pipeline_mode=pl.Buffered(1))
    return pl.pallas_call(
        kern,
        grid=(n_batch, n_kv, q_blocks),
        in_specs=[
            pl.BlockSpec((tq, gw), q_map),
            pl.BlockSpec((seq_len, head_dim), lambda b, h, i: (b, k_col0 + h)),
            pl.BlockSpec((seq_len, head_dim), lambda b, h, i: (b, v_col0 + h)),
            pl.BlockSpec((1, head_dim), whole),
            pl.BlockSpec((1, head_dim), whole),
            pl.BlockSpec((seq_len, head_dim), whole, **once),
            pl.BlockSpec((seq_len, head_dim), whole, **once),
            pl.BlockSpec((1, gw), lambda b, h, i: (0, h)),
            conv_in(0), conv_in(1), conv_in(2),
            pl.BlockSpec((conv_w.shape[0], conv_cols), conv_par),
            pl.BlockSpec((1, conv_cols), conv_par),
            *cast_specs,
        ],
        out_specs=[pl.BlockSpec((tq, gw), q_map),
                   pl.BlockSpec((seq_len, conv_cols), lambda b, h, i: (b, h * q_blocks + i)),
                   *cast_specs],
        out_shape=[jax.ShapeDtypeStruct((m, attn_width), BF16), jax.ShapeDtypeStruct((m, conv_width), BF16),
                   *(jax.ShapeDtypeStruct(w.shape, BF16) for w in cast_weights)],
        scratch_shapes=[pltpu.VMEM((head_dim, seq_len), BF16), pltpu.VMEM((seq_len, 2 * head_dim), BF16)],
        compiler_params=pltpu.CompilerParams(
            dimension_semantics=("arbitrary", "arbitrary", "arbitrary"),
            vmem_limit_bytes=VMEM_LIMIT_BYTES),
        name="attention",
    )(z, z, z, q_g, k_g, cos, sin, attn_g, z, z, z, conv_w, conv_g, *cast_weights)


def _out_proj_kernel(x_ref, a_ref, c_ref, wa_ref, wc_ref, g_ref, o_ref, h_ref, ss_ref):
    acc = jnp.dot(a_ref[...], wa_ref[...], preferred_element_type=F32)
    acc += jnp.dot(c_ref[...], wc_ref[...], preferred_element_type=F32)
    x1 = x_ref[...] + acc
    o_ref[...] = x1
    h_ref[...] = (x1 * g_ref[...]).astype(BF16)
    part = jnp.sum(x1 * x1, axis=-1, keepdims=True)

    @pl.when(pl.program_id(1) == 0)
    def _():
        ss_ref[...] = part

    @pl.when(pl.program_id(1) > 0)
    def _():
        ss_ref[...] += part


def _out_proj(x2, a, c, w_out, mlp_g):
    m, d = x2.shape
    ka = a.shape[1]
    kc = c.shape[1]
    assert ka == kc and w_out.shape[0] == ka + kc
    tm = _largest_tile(m, 1024, 16)
    tn = _largest_tile(d, 1024, 128)
    return pl.pallas_call(
        _out_proj_kernel,
        grid=(m // tm, d // tn),
        in_specs=[
            pl.BlockSpec((tm, tn), lambda i, j: (i, j)),
            pl.BlockSpec((tm, ka), lambda i, j: (i, 0)),
            pl.BlockSpec((tm, kc), lambda i, j: (i, 0)),
            pl.BlockSpec((ka, tn), lambda i, j: (0, j)),
            pl.BlockSpec((kc, tn), lambda i, j: (1, j)),
            pl.BlockSpec((1, tn), lambda i, j: (0, j)),
        ],
        out_specs=[
            pl.BlockSpec((tm, tn), lambda i, j: (i, j)),
            pl.BlockSpec((tm, tn), lambda i, j: (i, j)),
            pl.BlockSpec((tm, 1), lambda i, j: (i, 0)),
        ],
        out_shape=[jax.ShapeDtypeStruct((m, d), F32), jax.ShapeDtypeStruct((m, d), BF16),
                   jax.ShapeDtypeStruct((m, 1), F32)],
        compiler_params=pltpu.CompilerParams(
            dimension_semantics=("parallel", "arbitrary"), vmem_limit_bytes=VMEM_LIMIT_BYTES),
        name="out_proj",
    )(x2, a, c, w_out, w_out, mlp_g)


def _mlp_kernel(x_ref, h_ref, ss_ref, wu_ref, wd_ref, fg_ref, o_ref, *, slab_cols):
    f = pl.program_id(1)
    last = pl.num_programs(1) - 1

    def act():
        u = jnp.dot(h_ref[...], wu_ref[...], preferred_element_type=F32)
        return jnp.square(jnp.maximum(u, 0.0)).astype(BF16)

    def down(a, cols):
        return jnp.dot(a, wd_ref[:, cols], preferred_element_type=F32)

    d = o_ref.shape[1]
    slabs = [slice(n, n + slab_cols) for n in range(0, d, slab_cols)]

    @pl.when(f == 0)
    def _():
        a = act()
        for cols in slabs:
            o_ref[:, cols] = down(a, cols)

    @pl.when((f > 0) & (f < last))
    def _():
        a = act()
        for cols in slabs:
            o_ref[:, cols] += down(a, cols)

    @pl.when(f == last)
    def _():
        a = act()
        r2 = 1.0 / (ss_ref[...] * (1.0 / d) + EPS)
        for cols in slabs:
            o_ref[:, cols] = x_ref[:, cols] + r2 * (o_ref[:, cols] + down(a, cols))
        y = o_ref[...]
        o_ref[...] = y * _rms_scale(y) * fg_ref[...]


def _mlp(x1, h, ss, w_up, w_down, final_g):
    m, d = x1.shape
    d_ff = w_up.shape[1]
    tm = _largest_tile(m, 512, 16)
    tf = _largest_tile(d_ff, 512, 128)
    assert d_ff // tf >= 2
    return pl.pallas_call(
        functools.partial(_mlp_kernel, slab_cols=_largest_tile(d, 1024, 128)),
        grid=(m // tm, d_ff // tf),
        in_specs=[
            pl.BlockSpec((tm, d), lambda i, f: (i, 0)),
            pl.BlockSpec((tm, d), lambda i, f: (i, 0)),
            pl.BlockSpec((tm, 1), lambda i, f: (i, 0)),
            pl.BlockSpec((d, tf), lambda i, f: (0, f)),
            pl.BlockSpec((tf, d), lambda i, f: (f, 0)),
            pl.BlockSpec((1, d), lambda i, f: (0, 0)),
        ],
        out_specs=pl.BlockSpec((tm, d), lambda i, f: (i, 0)),
        out_shape=jax.ShapeDtypeStruct((m, d), F32),
        compiler_params=pltpu.CompilerParams(
            dimension_semantics=("parallel", "arbitrary"), vmem_limit_bytes=VMEM_LIMIT_BYTES),
        name="mlp",
    )(x1, h, ss, w_up, w_down, final_g)


def _trunk(x, w_in, q_g, k_g, conv_w, attn_g, conv_g, mix_g, mlp_g, final_g, cos, sin, late_weights, cast_late):
    n_batch, seq_len, d = x.shape
    attn_width = attn_g.shape[-1]
    conv_width = conv_g.shape[-1]
    kv_width = (w_in.shape[1] - attn_width - 3 * conv_width) // 2
    x2 = x.reshape(n_batch * seq_len, d)
    w_out, w_up, w_down = late_weights
    z, *cast = _in_proj(x2, mix_g, w_in, (w_out, w_down) if cast_late else ())
    if cast_late:
        w_out, w_down = cast
    a, c, *cast = _attention(z, q_g, k_g, cos, sin, attn_g, conv_w, conv_g, (w_up,) if cast_late else (),
                             n_batch=n_batch, seq_len=seq_len, attn_width=attn_width, kv_width=kv_width)
    if cast_late:
        (w_up,) = cast
    x1, h, ss = _out_proj(x2, a, c, w_out, mlp_g)
    y = _mlp(x1, h, ss, w_up, w_down, final_g)
    return y.reshape(n_batch, seq_len, d), (w_out, w_up, w_down)


def kernel(x_prompt, x_sample, w_in, q_norm, k_norm, conv_w, attn_grp_norm, conv_grp_norm, w_out,
           mix_norm, mlp_norm, w_up, w_down, final_norm):
    assert w_in.shape[0] == 1, "single-layer trunk"
    head_dim = q_norm.shape[-1]
    params = (w_in[0].astype(BF16), q_norm, k_norm, conv_w[0], attn_grp_norm, conv_grp_norm,
              mix_norm, mlp_norm, final_norm.reshape(1, -1))
    late_weights, cast_late = (w_out[0], w_up[0], w_down[0]), True
    outs = []
    for x in (x_prompt, x_sample):
        cos, sin = _rope_tables(x.shape[1], head_dim)
        y, late_weights = _trunk(x, *params, cos, sin, late_weights, cast_late)
        cast_late = False
        outs.append(y)
    return tuple(outs)
```

```python
import functools

import jax
import jax.numpy as jnp
from jax import lax
from jax.experimental import pallas as pl
from jax.experimental.pallas import tpu as pltpu

EPS = 1e-6
GRID_W = 64
ROPE_THETA = 10000.0
N_KV_HEADS = 4
CONV_GROUPS = 16
LOG2_E = 1.4426950408889634

V7X_VMEM_BYTES = 64 * 1024 * 1024
VMEM_LIMIT_BYTES = V7X_VMEM_BYTES - 2 * 1024 * 1024

F32 = jnp.float32
BF16 = jnp.bfloat16
BF16_SUBLANES = 16


def _largest_tile(dim, target, quantum):
    t = min(target, dim)
    t -= t % quantum
    while t > quantum and dim % t:
        t -= quantum
    assert t >= quantum and dim % t == 0, (dim, target, quantum)
    return t


def _rope_tables(seq_len, head_dim):
    axis_dim = head_dim // 2
    n_rows = seq_len // GRID_W
    row = jnp.repeat(jnp.arange(n_rows, dtype=F32), GRID_W)
    col = jnp.tile(jnp.arange(GRID_W, dtype=F32), n_rows)
    freqs = ROPE_THETA ** (-jnp.arange(0, axis_dim, 2, dtype=F32) / axis_dim)
    ang_r = row[:, None] * freqs
    ang_c = col[:, None] * freqs
    cos = jnp.concatenate([jnp.cos(ang_r), jnp.cos(ang_r), jnp.cos(ang_c), jnp.cos(ang_c)], axis=-1)
    sin = jnp.concatenate([-jnp.sin(ang_r), jnp.sin(ang_r), -jnp.sin(ang_c), jnp.sin(ang_c)], axis=-1)
    return cos, sin


def _rms_scale(x):
    return lax.rsqrt(jnp.mean(x * x, axis=-1, keepdims=True) + EPS)


def _in_proj_kernel(xs_ref, g_ref, w_ref, *refs, n_cast, n_slices):
    cast_in, (z_ref, *cast_out), (h_ref, r_ref) = refs[:n_cast], refs[n_cast:2 * n_cast + 1], refs[2 * n_cast + 1:]
    i = pl.program_id(0)
    j = pl.program_id(1)
    slice_rows = xs_ref.shape[0]

    def side_jobs():
        sl = jnp.where(i == pl.num_programs(0) - 1, n_slices - 1, jnp.clip(j - 1, 0, n_slices - 1))
        rows = pl.ds(pl.multiple_of(sl * slice_rows, slice_rows), slice_rows)
        xf = xs_ref[...]
        h_ref[i % 2, rows, :] = (xf * g_ref[...]).astype(BF16)
        r_ref[i % 2, rows, :] = _rms_scale(xf)
        for w_in_ref, w_out_ref in zip(cast_in, cast_out):
            w_out_ref[...] = w_in_ref[...].astype(BF16)

    @pl.when(i == 0)
    def _():
        z_ref[...] = jnp.zeros(z_ref.shape, BF16)
        side_jobs()

    @pl.when(i > 0)
    def _():
        slot = (i - 1) % 2
        acc = jnp.dot(h_ref[slot], w_ref[...], preferred_element_type=F32)
        z_ref[...] = (acc * r_ref[slot]).astype(BF16)
        side_jobs()


def _in_proj(x2, mix_g, w_in, cast_weights):
    m, d = x2.shape
    n = w_in.shape[1]
    tm = _largest_tile(m, 512, 16)
    tn = _largest_tile(n, 1024, 128)
    n_i, n_j = m // tm, n // tn
    assert n_j >= 2
    n_slices = 1 << ((n_j - 1).bit_length() - 1)
    assert tm % (n_slices * BF16_SUBLANES) == 0
    slice_of = lambda i, j: jnp.where(i == n_i, n_slices - 1, jnp.clip(j - 1, 0, n_slices - 1))
    tile_of = lambda i: jnp.minimum(i, n_i - 1)
    done_of = lambda i: jnp.where(i == 0, n_i, i - 1)
    cast_map = lambda i, j: (tile_of(i) * n_slices + slice_of(i, j), 0)
    cast_specs = []
    for w in cast_weights:
        assert w.shape[0] % (n_i * n_slices * BF16_SUBLANES) == 0, (w.shape, n_i, n_slices)
        cast_specs.append(pl.BlockSpec((w.shape[0] // (n_i * n_slices), w.shape[1]), cast_map))
    return pl.pallas_call(
        functools.partial(_in_proj_kernel, n_cast=len(cast_weights), n_slices=n_slices),
        grid=(n_i + 1, n_j),
        in_specs=[
            pl.BlockSpec((tm // n_slices, d), lambda i, j: (tile_of(i) * n_slices + slice_of(i, j), 0)),
            pl.BlockSpec((1, d), lambda i, j: (0, 0)),
            pl.BlockSpec((d, tn), lambda i, j: (0, jnp.where(i == 0, 0, j))),
            *cast_specs,
        ],
        out_specs=[pl.BlockSpec((tm, tn), lambda i, j: (done_of(i), j)), *cast_specs],
        out_shape=[jax.ShapeDtypeStruct((m + tm, n), BF16),
                   *(jax.ShapeDtypeStruct(w.shape, BF16) for w in cast_weights)],
        scratch_shapes=[pltpu.VMEM((2, tm, d), BF16), pltpu.VMEM((2, tm, 1), F32)],
        compiler_params=pltpu.CompilerParams(
            dimension_semantics=("arbitrary", "arbitrary"), vmem_limit_bytes=VMEM_LIMIT_BYTES),
        name="in_proj",
    )(x2, mix_g, w_in, *cast_weights)


def _norm_rope(a, gain, cos, sin):
    quarter = a.shape[1] // 4
    a = a * _rms_scale(a) * gain
    lane = lax.broadcasted_iota(jnp.int32, a.shape, 1)
    up = pltpu.roll(a, a.shape[1] - quarter, axis=1)
    down = pltpu.roll(a, quarter, axis=1)
    return a * cos + jnp.where((lane & quarter) == 0, up, down) * sin


def _gated_conv(b, c, u, w, gain):
    s = b.shape[0]
    cu = c * u
    t = lax.broadcasted_iota(jnp.int32, cu.shape, 0)
    prev = jnp.where(t == 0, 0.0, pltpu.roll(cu, 1, axis=0))
    nxt = jnp.where(t == s - 1, 0.0, pltpu.roll(cu, s - 1, axis=0))
    y = b * (w[0:1] * prev + w[1:2] * cu + w[2:3] * nxt)
    return y * _rms_scale(y) * gain


def _attention_kernel(q_ref, k_ref, v_ref, qg_ref, kg_ref, cos_ref, sin_ref, g_ref,
                      cb_ref, cc_ref, cu_ref, cw_ref, cg_ref, *refs,
                      group, head_dim, unit_rows, ramp_rows, conv_group_dim, n_cast):
    cast_in, (o_ref, co_ref, *cast_out), (kt_ref, vx_ref) = (
        refs[:n_cast], refs[n_cast:2 * n_cast + 2], refs[2 * n_cast + 2:])
    tq = q_ref.shape[0]
    g = pl.program_id(0)
    i = pl.program_id(1)
    n_blocks = pl.num_programs(1)
    blk = jnp.where(g == pl.num_programs(0) - 1, n_blocks - 1, i)

    def side_jobs():
        rows = pl.ds(pl.multiple_of(blk * tq, tq), tq)
        k = _norm_rope(k_ref[rows, :].astype(F32), kg_ref[...], cos_ref[rows, :], sin_ref[rows, :])
        kt_ref[g % 2, blk] = k.astype(BF16).T
        vx_ref[g % 2, rows, :head_dim] = v_ref[rows, :]
        vx_ref[g % 2, rows, head_dim:] = jnp.ones((tq, head_dim), BF16)
        for c0 in range(0, cb_ref.shape[1], conv_group_dim):
            cc = slice(c0, c0 + conv_group_dim)
            y = _gated_conv(cb_ref[:, cc].astype(F32), cc_ref[:, cc].astype(F32), cu_ref[:, cc].astype(F32),
                            cw_ref[:, cc], cg_ref[:, cc])
            co_ref[:, cc] = y.astype(BF16)
        for w_in_ref, w_out_ref in zip(cast_in, cast_out):
            w_out_ref[...] = w_in_ref[...].astype(BF16)

    @pl.when(g == 0)
    def _():
        side_jobs()

    @pl.when(g > 0)
    def _():
        slot = (g - 1) % 2
        kt = jnp.concatenate([kt_ref[slot, c] for c in range(kt_ref.shape[1])], axis=1)
        vx = vx_ref[slot]
        q_scale = float(head_dim) ** -0.5 * LOG2_E
        for h in range(group):
            cols = slice(h * head_dim, (h + 1) * head_dim)
            bounds = list(range(0, tq + 1, unit_rows))
            if h == 0 and unit_rows > ramp_rows:
                bounds.insert(1, ramp_rows)
            if h == group - 1 and unit_rows > ramp_rows:
                bounds.insert(-1, tq - ramp_rows)
            for r, r_end in zip(bounds[:-1], bounds[1:]):
                rows = slice(r, r_end)
                pos = pl.ds(pl.multiple_of(i * tq + r, ramp_rows), r_end - r)
                q = _norm_rope(q_ref[rows, cols].astype(F32), qg_ref[...], cos_ref[pos, :], sin_ref[pos, :])
                s = jnp.dot((q * q_scale).astype(BF16), kt, preferred_element_type=F32)
                m = jnp.max(s, axis=-1, keepdims=True)
                p = jnp.exp2(s - m).astype(BF16)
                ol = jnp.dot(p, vx, preferred_element_type=F32)
                o = ol[:, :head_dim] / ol[:, head_dim:]
                o_ref[rows, cols] = (o * _rms_scale(o) * g_ref[:, cols]).astype(BF16)
        side_jobs()


def _attention(z, q_g, k_g, cos, sin, attn_g, conv_w, conv_g, cast_weights, *,
               n_batch, seq_len, attn_width, kv_width):
    m = n_batch * seq_len
    head_dim = q_g.shape[-1]
    conv_width = conv_g.shape[-1]
    n_kv = N_KV_HEADS
    group = attn_width // head_dim // n_kv
    assert kv_width == n_kv * head_dim
    tq = _largest_tile(seq_len, 1024, 16)
    q_blocks = seq_len // tq
    gw = group * head_dim
    k_col0 = attn_width // head_dim
    v_col0 = (attn_width + kv_width) // head_dim
    steps_per_batch = n_kv * q_blocks
    conv_cols = conv_width // steps_per_batch
    conv_group_dim = conv_width // CONV_GROUPS
    conv_col0 = attn_width + 2 * kv_width
    assert conv_width % steps_per_batch == 0 and conv_cols % conv_group_dim == 0 and conv_col0 % conv_cols == 0
    n_conv = conv_width // conv_cols
    n_groups = n_batch * n_kv
    kern = functools.partial(_attention_kernel, group=group, head_dim=head_dim,
                             unit_rows=_largest_tile(tq, 256, 16), ramp_rows=_largest_tile(tq, 64, 16),
                             conv_group_dim=conv_group_dim, n_cast=len(cast_weights))
    prep = lambda g: jnp.minimum(g, n_groups - 1)
    prep_blk = lambda g, i: jnp.where(g == n_groups, q_blocks - 1, i)
    done = lambda g: jnp.maximum(g - 1, 0)
    cast_specs = []
    for w in cast_weights:
        assert w.shape[0] % (n_groups * q_blocks * BF16_SUBLANES) == 0, (w.shape, n_groups, q_blocks)
        cast_specs.append(pl.BlockSpec((w.shape[0] // (n_groups * q_blocks), w.shape[1]),
                                       lambda g, i: (prep(g) * q_blocks + prep_blk(g, i), 0)))
    whole = lambda g, i: (0, 0)
    q_map = lambda g, i: ((done(g) // n_kv) * q_blocks + i, done(g) % n_kv)
    o_map = lambda g, i: ((done(g) // n_kv) * q_blocks + jnp.where(g == 0, 0, i), done(g) % n_kv)
    conv_blk = lambda g, i: (prep(g) % n_kv) * q_blocks + prep_blk(g, i)

    def conv_in(part):
        return pl.BlockSpec((seq_len, conv_cols),
                            lambda g, i: (prep(g) // n_kv, conv_col0 // conv_cols + part * n_conv + conv_blk(g, i)))

    once = dict(pipeline_mode=pl.Buffered(1))
    return pl.pallas_call(
        kern,
        grid=(n_groups + 1, q_blocks),
        in_specs=[
            pl.BlockSpec((tq, gw), q_map),
            pl.BlockSpec((seq_len, head_dim), lambda g, i: (prep(g) // n_kv, k_col0 + prep(g) % n_kv)),
            pl.BlockSpec((seq_len, head_dim), lambda g, i: (prep(g) // n_kv, v_col0 + prep(g) % n_kv)),
            pl.BlockSpec((1, head_dim), whole),
            pl.BlockSpec((1, head_dim), whole),
            pl.BlockSpec((seq_len, head_dim), whole, **once),
            pl.BlockSpec((seq_len, head_dim), whole, **once),
            pl.BlockSpec((1, gw), lambda g, i: (0, done(g) % n_kv)),
            conv_in(0), conv_in(1), conv_in(2),
            pl.BlockSpec((conv_w.shape[0], conv_cols), lambda g, i: (0, conv_blk(g, i))),
            pl.BlockSpec((1, conv_cols), lambda g, i: (0, conv_blk(g, i))),
            *cast_specs,
        ],
        out_specs=[pl.BlockSpec((tq, gw), o_map),
                   pl.BlockSpec((seq_len, conv_cols), lambda g, i: (prep(g) // n_kv, conv_blk(g, i))),
                   *cast_specs],
        out_shape=[jax.ShapeDtypeStruct((m, attn_width), BF16), jax.ShapeDtypeStruct((m, conv_width), BF16),
                   *(jax.ShapeDtypeStruct(w.shape, BF16) for w in cast_weights)],
        scratch_shapes=[pltpu.VMEM((2, q_blocks, head_dim, tq), BF16),
                        pltpu.VMEM((2, seq_len, 2 * head_dim), BF16)],
        compiler_params=pltpu.CompilerParams(
            dimension_semantics=("arbitrary", "arbitrary"),
            vmem_limit_bytes=VMEM_LIMIT_BYTES),
        name="attention",
    )(z, z, z, q_g, k_g, cos, sin, attn_g, z, z, z, conv_w, conv_g, *cast_weights)


def _out_proj_kernel(x_ref, a_ref, c_ref, wa_ref, wc_ref, g_ref, o_ref, h_ref, ss_ref):
    acc = jnp.dot(a_ref[...], wa_ref[...], preferred_element_type=F32)
    acc += jnp.dot(c_ref[...], wc_ref[...], preferred_element_type=F32)
    x1 = x_ref[...] + acc
    o_ref[...] = x1
    h_ref[...] = (x1 * g_ref[...]).astype(BF16)
    part = jnp.sum(x1 * x1, axis=-1, keepdims=True)

    @pl.when(pl.program_id(1) == 0)
    def _():
        ss_ref[...] = part

    @pl.when(pl.program_id(1) > 0)
    def _():
        ss_ref[...] += part


def _out_proj(x2, a, c, w_out, mlp_g):
    m, d = x2.shape
    ka = a.shape[1]
    kc = c.shape[1]
    assert ka == kc and w_out.shape[0] == ka + kc
    tm = _largest_tile(m, 1024, 16)
    tn = _largest_tile(d, 1024, 128)
    return pl.pallas_call(
        _out_proj_kernel,
        grid=(m // tm, d // tn),
        in_specs=[
            pl.BlockSpec((tm, tn), lambda i, j: (i, j)),
            pl.BlockSpec((tm, ka), lambda i, j: (i, 0)),
            pl.BlockSpec((tm, kc), lambda i, j: (i, 0)),
            pl.BlockSpec((ka, tn), lambda i, j: (0, j)),
            pl.BlockSpec((kc, tn), lambda i, j: (1, j)),
            pl.BlockSpec((1, tn), lambda i, j: (0, j)),
        ],
        out_specs=[
            pl.BlockSpec((tm, tn), lambda i, j: (i, j)),
            pl.BlockSpec((tm, tn), lambda i, j: (i, j)),
            pl.BlockSpec((tm, 1), lambda i, j: (i, 0)),
        ],
        out_shape=[jax.ShapeDtypeStruct((m, d), F32), jax.ShapeDtypeStruct((m, d), BF16),
                   jax.ShapeDtypeStruct((m, 1), F32)],
        compiler_params=pltpu.CompilerParams(
            dimension_semantics=("parallel", "arbitrary"), vmem_limit_bytes=VMEM_LIMIT_BYTES),
        name="out_proj",
    )(x2, a, c, w_out, w_out, mlp_g)


def _mlp_kernel(x_ref, h_ref, ss_ref, wu_ref, wd_ref, fg_ref, o_ref, *, slab_cols):
    f = pl.program_id(1)
    last = pl.num_programs(1) - 1

    def act():
        u = jnp.dot(h_ref[...], wu_ref[...], preferred_element_type=F32)
        return jnp.square(jnp.maximum(u, 0.0)).astype(BF16)

    def down(a, cols):
        return jnp.dot(a, wd_ref[:, cols], preferred_element_type=F32)

    d = o_ref.shape[1]
    slabs = [slice(n, n + slab_cols) for n in range(0, d, slab_cols)]

    @pl.when(f == 0)
    def _():
        a = act()
        for cols in slabs:
            o_ref[:, cols] = down(a, cols)

    @pl.when((f > 0) & (f < last))
    def _():
        a = act()
        for cols in slabs:
            o_ref[:, cols] += down(a, cols)

    @pl.when(f == last)
    def _():
        a = act()
        r2 = 1.0 / (ss_ref[...] * (1.0 / d) + EPS)
        for cols in slabs:
            o_ref[:, cols] = x_ref[:, cols] + r2 * (o_ref[:, cols] + down(a, cols))
        y = o_ref[...]
        o_ref[...] = y * _rms_scale(y) * fg_ref[...]


def _mlp(x1, h, ss, w_up, w_down, final_g):
    m, d = x1.shape
    d_ff = w_up.shape[1]
    tm = _largest_tile(m, 512, 16)
    tf = _largest_tile(d_ff, 512, 128)
    assert d_ff // tf >= 2
    return pl.pallas_call(
        functools.partial(_mlp_kernel, slab_cols=_largest_tile(d, 1024, 128)),
        grid=(m // tm, d_ff // tf),
        in_specs=[
            pl.BlockSpec((tm, d), lambda i, f: (i, 0)),
            pl.BlockSpec((tm, d), lambda i, f: (i, 0)),
            pl.BlockSpec((tm, 1), lambda i, f: (i, 0)),
            pl.BlockSpec((d, tf), lambda i, f: (0, f)),
            pl.BlockSpec((tf, d), lambda i, f: (f, 0)),
            pl.BlockSpec((1, d), lambda i, f: (0, 0)),
        ],
        out_specs=pl.BlockSpec((tm, d), lambda i, f: (i, 0)),
        out_shape=jax.ShapeDtypeStruct((m, d), F32),
        compiler_params=pltpu.CompilerParams(
            dimension_semantics=("parallel", "arbitrary"), vmem_limit_bytes=VMEM_LIMIT_BYTES),
        name="mlp",
    )(x1, h, ss, w_up, w_down, final_g)


def _trunk(x, w_in, q_g, k_g, conv_w, attn_g, conv_g, mix_g, mlp_g, final_g, cos, sin, late_weights, cast_late):
    n_batch, seq_len, d = x.shape
    attn_width = attn_g.shape[-1]
    conv_width = conv_g.shape[-1]
    kv_width = (w_in.shape[1] - attn_width - 3 * conv_width) // 2
    x2 = x.reshape(n_batch * seq_len, d)
    w_out, w_up, w_down = late_weights
    z, *cast = _in_proj(x2, mix_g, w_in, (w_out, w_down) if cast_late else ())
    if cast_late:
        w_out, w_down = cast
    a, c, *cast = _attention(z, q_g, k_g, cos, sin, attn_g, conv_w, conv_g, (w_up,) if cast_late else (),
                             n_batch=n_batch, seq_len=seq_len, attn_width=attn_width, kv_width=kv_width)
    if cast_late:
        (w_up,) = cast
    x1, h, ss = _out_proj(x2, a, c, w_out, mlp_g)
    y = _mlp(x1, h, ss, w_up, w_down, final_g)
    return y.reshape(n_batch, seq_len, d), (w_out, w_up, w_down)


def kernel(x_prompt, x_sample, w_in, q_norm, k_norm, conv_w, attn_grp_norm, conv_grp_norm, w_out,
           mix_norm, mlp_norm, w_up, w_down, final_norm):
    assert w_in.shape[0] == 1, "single-layer trunk"
    head_dim = q_norm.shape[-1]
    params = (w_in[0].astype(BF16), q_norm, k_norm, conv_w[0], attn_grp_norm, conv_grp_norm,
              mix_norm, mlp_norm, final_norm.reshape(1, -1))
    late_weights, cast_late = (w_out[0], w_up[0], w_down[0]), True
    outs = []
    for x in (x_prompt, x_sample):
        cos, sin = _rope_tables(x.shape[1], head_dim)
        y, late_weights = _trunk(x, *params, cos, sin, late_weights, cast_late)
        cast_late = False
        outs.append(y)
    return tuple(outs)
```

```python
import functools

import jax
import jax.numpy as jnp
from jax import lax
from jax.experimental import pallas as pl
from jax.experimental.pallas import tpu as pltpu

EPS = 1e-6
GRID_W = 64
ROPE_THETA = 10000.0
N_KV_HEADS = 4
CONV_GROUPS = 16
LOG2_E = 1.4426950408889634

V7X_VMEM_BYTES = 64 * 1024 * 1024
VMEM_LIMIT_BYTES = V7X_VMEM_BYTES - 2 * 1024 * 1024

F32 = jnp.float32
BF16 = jnp.bfloat16
BF16_SUBLANES = 16


def _largest_tile(dim, target, quantum):
    t = min(target, dim)
    t -= t % quantum
    while t > quantum and dim % t:
        t -= quantum
    assert t >= quantum and dim % t == 0, (dim, target, quantum)
    return t


def _rope_tables(seq_len, head_dim):
    axis_dim = head_dim // 2
    n_rows = seq_len // GRID_W
    row = jnp.repeat(jnp.arange(n_rows, dtype=F32), GRID_W)
    col = jnp.tile(jnp.arange(GRID_W, dtype=F32), n_rows)
    freqs = ROPE_THETA ** (-jnp.arange(0, axis_dim, 2, dtype=F32) / axis_dim)
    ang_r = row[:, None] * freqs
    ang_c = col[:, None] * freqs
    cos = jnp.concatenate([jnp.cos(ang_r), jnp.cos(ang_r), jnp.cos(ang_c), jnp.cos(ang_c)], axis=-1)
    sin = jnp.concatenate([-jnp.sin(ang_r), jnp.sin(ang_r), -jnp.sin(ang_c), jnp.sin(ang_c)], axis=-1)
    return cos, sin


def _rms_scale(x):
    return lax.rsqrt(jnp.mean(x * x, axis=-1, keepdims=True) + EPS)


def _in_proj_kernel(xs_ref, g_ref, w_ref, *refs, n_cast, n_slices):
    cast_in, (z_ref, *cast_out), (h_ref, r_ref) = refs[:n_cast], refs[n_cast:2 * n_cast + 1], refs[2 * n_cast + 1:]
    i = pl.program_id(0)
    j = pl.program_id(1)
    slice_rows = xs_ref.shape[0]

    def side_jobs():
        sl = jnp.where(i == pl.num_programs(0) - 1, n_slices - 1, jnp.clip(j - 1, 0, n_slices - 1))
        rows = pl.ds(pl.multiple_of(sl * slice_rows, slice_rows), slice_rows)
        xf = xs_ref[...]
        h_ref[i % 2, rows, :] = (xf * g_ref[...]).astype(BF16)
        r_ref[i % 2, rows, :] = _rms_scale(xf)
        for w_in_ref, w_out_ref in zip(cast_in, cast_out):
            w_out_ref[...] = w_in_ref[...].astype(BF16)

    @pl.when(i == 0)
    def _():
        side_jobs()

    @pl.when(i > 0)
    def _():
        slot = (i - 1) % 2
        acc = jnp.dot(h_ref[slot], w_ref[...], preferred_element_type=F32)
        z_ref[...] = (acc * r_ref[slot]).astype(BF16)
        side_jobs()


def _in_proj(x2, mix_g, w_in, cast_weights):
    m, d = x2.shape
    n = w_in.shape[1]
    tm = _largest_tile(m, 512, 16)
    tn = _largest_tile(n, 1024, 128)
    n_i, n_j = m // tm, n // tn
    assert n_j >= 2
    n_slices = 1 << ((n_j - 1).bit_length() - 1)
    assert tm % (n_slices * BF16_SUBLANES) == 0
    slice_of = lambda i, j: jnp.where(i == n_i, n_slices - 1, jnp.clip(j - 1, 0, n_slices - 1))
    tile_of = lambda i: jnp.minimum(i, n_i - 1)
    done_of = lambda i: jnp.maximum(i - 1, 0)
    cast_map = lambda i, j: (tile_of(i) * n_slices + slice_of(i, j), 0)
    cast_specs = []
    for w in cast_weights:
        assert w.shape[0] % (n_i * n_slices * BF16_SUBLANES) == 0, (w.shape, n_i, n_slices)
        cast_specs.append(pl.BlockSpec((w.shape[0] // (n_i * n_slices), w.shape[1]), cast_map))
    return pl.pallas_call(
        functools.partial(_in_proj_kernel, n_cast=len(cast_weights), n_slices=n_slices),
        grid=(n_i + 1, n_j),
        in_specs=[
            pl.BlockSpec((tm // n_slices, d), lambda i, j: (tile_of(i) * n_slices + slice_of(i, j), 0)),
            pl.BlockSpec((1, d), lambda i, j: (0, 0)),
            pl.BlockSpec((d, tn), lambda i, j: (0, jnp.where(i == 0, 0, j))),
            *cast_specs,
        ],
        out_specs=[pl.BlockSpec((tm, tn), lambda i, j: (done_of(i), jnp.where(i == 0, 0, j))), *cast_specs],
        out_shape=[jax.ShapeDtypeStruct((m, n), BF16),
                   *(jax.ShapeDtypeStruct(w.shape, BF16) for w in cast_weights)],
        scratch_shapes=[pltpu.VMEM((2, tm, d), BF16), pltpu.VMEM((2, tm, 1), F32)],
        compiler_params=pltpu.CompilerParams(
            dimension_semantics=("arbitrary", "arbitrary"), vmem_limit_bytes=VMEM_LIMIT_BYTES),
        name="in_proj",
    )(x2, mix_g, w_in, *cast_weights)


def _norm_rope(a, gain, cos, sin):
    quarter = a.shape[1] // 4
    a = a * _rms_scale(a) * gain
    lane = lax.broadcasted_iota(jnp.int32, a.shape, 1)
    up = pltpu.roll(a, a.shape[1] - quarter, axis=1)
    down = pltpu.roll(a, quarter, axis=1)
    return a * cos + jnp.where((lane & quarter) == 0, up, down) * sin


def _gated_conv(b, c, u, w, gain):
    s = b.shape[0]
    cu = c * u
    t = lax.broadcasted_iota(jnp.int32, cu.shape, 0)
    prev = jnp.where(t == 0, 0.0, pltpu.roll(cu, 1, axis=0))
    nxt = jnp.where(t == s - 1, 0.0, pltpu.roll(cu, s - 1, axis=0))
    y = b * (w[0:1] * prev + w[1:2] * cu + w[2:3] * nxt)
    return y * _rms_scale(y) * gain


def _attention_kernel(q_ref, k_ref, v_ref, qg_ref, kg_ref, cos_ref, sin_ref, g_ref,
                      cb_ref, cc_ref, cu_ref, cw_ref, cg_ref, *refs,
                      group, head_dim, unit_rows, ramp_rows, conv_group_dim, n_cast):
    cast_in, (o_ref, co_ref, *cast_out), (kt_ref, vx_ref) = (
        refs[:n_cast], refs[n_cast:2 * n_cast + 2], refs[2 * n_cast + 2:])
    seq_len = k_ref.shape[0]
    tq = q_ref.shape[0]
    i = pl.program_id(2)

    @pl.when(i == 0)
    def _():
        k = _norm_rope(k_ref[...].astype(F32), kg_ref[...], cos_ref[...], sin_ref[...])
        kt_ref[...] = k.astype(BF16).T
        vx_ref[:, :head_dim] = v_ref[...]
        vx_ref[:, head_dim:] = jnp.ones((seq_len, head_dim), BF16)

    q_scale = float(head_dim) ** -0.5 * LOG2_E
    for h in range(group):
        cols = slice(h * head_dim, (h + 1) * head_dim)
        bounds = list(range(0, tq + 1, unit_rows))
        if h == 0 and unit_rows > ramp_rows:
            bounds.insert(1, ramp_rows)
        if h == group - 1 and unit_rows > ramp_rows:
            bounds.insert(-1, tq - ramp_rows)
        for r, r_end in zip(bounds[:-1], bounds[1:]):
            rows = slice(r, r_end)
            pos = pl.ds(pl.multiple_of(i * tq + r, ramp_rows), r_end - r)
            q = _norm_rope(q_ref[rows, cols].astype(F32), qg_ref[...], cos_ref[pos, :], sin_ref[pos, :])
            s = jnp.dot((q * q_scale).astype(BF16), kt_ref[...], preferred_element_type=F32)
            m = jnp.max(s, axis=-1, keepdims=True)
            p = jnp.exp2(s - m).astype(BF16)
            ol = jnp.dot(p, vx_ref[...], preferred_element_type=F32)
            o = ol[:, :head_dim] / ol[:, head_dim:]
            o_ref[rows, cols] = (o * _rms_scale(o) * g_ref[:, cols]).astype(BF16)

    for c0 in range(0, cb_ref.shape[1], conv_group_dim):
        cc = slice(c0, c0 + conv_group_dim)
        y = _gated_conv(cb_ref[:, cc].astype(F32), cc_ref[:, cc].astype(F32), cu_ref[:, cc].astype(F32),
                        cw_ref[:, cc], cg_ref[:, cc])
        co_ref[:, cc] = y.astype(BF16)

    for w_in_ref, w_out_ref in zip(cast_in, cast_out):
        w_out_ref[...] = w_in_ref[...].astype(BF16)


def _attention(z, q_g, k_g, cos, sin, attn_g, conv_w, conv_g, cast_weights, *,
               n_batch, seq_len, attn_width, kv_width):
    m = n_batch * seq_len
    head_dim = q_g.shape[-1]
    conv_width = conv_g.shape[-1]
    n_kv = N_KV_HEADS
    group = attn_width // head_dim // n_kv
    assert kv_width == n_kv * head_dim
    tq = _largest_tile(seq_len, 1024, 16)
    q_blocks = seq_len // tq
    gw = group * head_dim
    k_col0 = attn_width // head_dim
    v_col0 = (attn_width + kv_width) // head_dim
    steps_per_batch = n_kv * q_blocks
    conv_cols = conv_width // steps_per_batch
    conv_group_dim = conv_width // CONV_GROUPS
    conv_col0 = attn_width + 2 * kv_width
    assert conv_width % steps_per_batch == 0 and conv_cols % conv_group_dim == 0 and conv_col0 % conv_cols == 0
    n_conv = conv_width // conv_cols
    kern = functools.partial(_attention_kernel, group=group, head_dim=head_dim,
                             unit_rows=_largest_tile(tq, 256, 16), ramp_rows=_largest_tile(tq, 64, 16),
                             conv_group_dim=conv_group_dim, n_cast=len(cast_weights))
    n_steps = n_batch * steps_per_batch
    step_map = lambda b, h, i: ((b * n_kv + h) * q_blocks + i, 0)
    cast_specs = []
    for w in cast_weights:
        assert w.shape[0] % (n_steps * BF16_SUBLANES) == 0, (w.shape, n_steps)
        cast_specs.append(pl.BlockSpec((w.shape[0] // n_steps, w.shape[1]), step_map))
    whole = lambda b, h, i: (0, 0)
    q_map = lambda b, h, i: (b * q_blocks + i, h)

    def conv_in(part):
        return pl.BlockSpec((seq_len, conv_cols),
                            lambda b, h, i: (b, conv_col0 // conv_cols + part * n_conv + h * q_blocks + i))

    conv_par = lambda b, h, i: (0, h * q_blocks + i)
    once = dict(pipeline_mode=pl.Buffered(1))
    return pl.pallas_call(
        kern,
        grid=(n_batch, n_kv, q_blocks),
        in_specs=[
            pl.BlockSpec((tq, gw), q_map),
            pl.BlockSpec((seq_len, head_dim), lambda b, h, i: (b, k_col0 + h)),
            pl.BlockSpec((seq_len, head_dim), lambda b, h, i: (b, v_col0 + h)),
            pl.BlockSpec((1, head_dim), whole),
            pl.BlockSpec((1, head_dim), whole),
            pl.BlockSpec((seq_len, head_dim), whole, **once),
            pl.BlockSpec((seq_len, head_dim), whole, **once),
            pl.BlockSpec((1, gw), lambda b, h, i: (0, h)),
            conv_in(0), conv_in(1), conv_in(2),
            pl.BlockSpec((conv_w.shape[0], conv_cols), conv_par),
            pl.BlockSpec((1, conv_cols), conv_par),
            *cast_specs,
        ],
        out_specs=[pl.BlockSpec((tq, gw), q_map),
                   pl.BlockSpec((seq_len, conv_cols), lambda b, h, i: (b, h * q_blocks + i)),
                   *cast_specs],
        out_shape=[jax.ShapeDtypeStruct((m, attn_width), BF16), jax.ShapeDtypeStruct((m, conv_width), BF16),
                   *(jax.ShapeDtypeStruct(w.shape, BF16) for w in cast_weights)],
        scratch_shapes=[pltpu.VMEM((head_dim, seq_len), BF16), pltpu.VMEM((seq_len, 2 * head_dim), BF16)],
        compiler_params=pltpu.CompilerParams(
            dimension_semantics=("arbitrary", "arbitrary", "arbitrary"),
            vmem_limit_bytes=VMEM_LIMIT_BYTES),
        name="attention",
    )(z, z, z, q_g, k_g, cos, sin, attn_g, z, z, z, conv_w, conv_g, *cast_weights)


def _out_proj_kernel(x_ref, a_ref, c_ref, wa_ref, wc_ref, g_ref, o_ref, h_ref, ss_ref):
    acc = jnp.dot(a_ref[...], wa_ref[...], preferred_element_type=F32)
    acc += jnp.dot(c_ref[...], wc_ref[...], preferred_element_type=F32)
    x1 = x_ref[...] + acc
    o_ref[...] = x1
    h_ref[...] = (x1 * g_ref[...]).astype(BF16)
    part = jnp.sum(x1 * x1, axis=-1, keepdims=True)

    @pl.when(pl.program_id(1) == 0)
    def _():
        ss_ref[...] = part

    @pl.when(pl.program_id(1) > 0)
    def _():
        ss_ref[...] += part


def _out_proj(x2, a, c, w_out, mlp_g):
    m, d = x2.shape
    ka = a.shape[1]
    kc = c.shape[1]
    assert ka == kc and w_out.shape[0] == ka + kc
    tm = _largest_tile(m, 1024, 16)
    tn = _largest_tile(d, 1024, 128)
    return pl.pallas_call(
        _out_proj_kernel,
        grid=(m // tm, d // tn),
        in_specs=[
            pl.BlockSpec((tm, tn), lambda i, j: (i, j)),
            pl.BlockSpec((tm, ka), lambda i, j: (i, 0)),
            pl.BlockSpec((tm, kc), lambda i, j: (i, 0)),
            pl.BlockSpec((ka, tn), lambda i, j: (0, j)),
            pl.BlockSpec((kc, tn), lambda i, j: (1, j)),
            pl.BlockSpec((1, tn), lambda i, j: (0, j)),
        ],
        out_specs=[
            pl.BlockSpec((tm, tn), lambda i, j: (i, j)),
            pl.BlockSpec((tm, tn), lambda i, j: (i, j)),
            pl.BlockSpec((tm, 1), lambda i, j: (i, 0)),
        ],
        out_shape=[jax.ShapeDtypeStruct((m, d), F32), jax.ShapeDtypeStruct((m, d), BF16),
                   jax.ShapeDtypeStruct((m, 1), F32)],
        compiler_params=pltpu.CompilerParams(
            dimension_semantics=("parallel", "arbitrary"), vmem_limit_bytes=VMEM_LIMIT_BYTES),
        name="out_proj",
    )(x2, a, c, w_out, w_out, mlp_g)


def _mlp_kernel(x_ref, h_ref, ss_ref, wu_ref, wd_ref, fg_ref, o_ref, *, slab_cols):
    f = pl.program_id(1)
    last = pl.num_programs(1) - 1

    def act():
        u = jnp.dot(h_ref[...], wu_ref[...], preferred_element_type=F32)
        return jnp.square(jnp.maximum(u, 0.0)).astype(BF16)

    def down(a, cols):
        return jnp.dot(a, wd_ref[:, cols], preferred_element_type=F32)

    d = o_ref.shape[1]
    slabs = [slice(n, n + slab_cols) for n in range(0, d, slab_cols)]

    @pl.when(f == 0)
    def _():
        a = act()
        for cols in slabs:
            o_ref[:, cols] = down(a, cols)

    @pl.when((f > 0) & (f < last))
    def _():
        a = act()
        for cols in slabs:
            o_ref[:, cols] += down(a, cols)

    @pl.when(f == last)
    def _():
        a = act()
        r2 = 1.0 / (ss_ref[...] * (1.0 / d) + EPS)
        for cols in slabs:
            o_ref[:, cols] = x_ref[:, cols] + r2 * (o_ref[:, cols] + down(a, cols))
        y = o_ref[...]
        o_ref[...] = y * _rms_scale(y) * fg_ref[...]


def _mlp(x1, h, ss, w_up, w_down, final_g):
    m, d = x1.shape
    d_ff = w_up.shape[1]
    tm = _largest_tile(m, 512, 16)
    tf = _largest_tile(d_ff, 512, 128)
    assert d_ff // tf >= 2
    return pl.pallas_call(
        functools.partial(_mlp_kernel, slab_cols=_largest_tile(d, 1024, 128)),
        grid=(m // tm, d_ff // tf),
        in_specs=[
            pl.BlockSpec((tm, d), lambda i, f: (i, 0)),
            pl.BlockSpec((tm, d), lambda i, f: (i, 0)),
            pl.BlockSpec((tm, 1), lambda i, f: (i, 0)),
            pl.BlockSpec((d, tf), lambda i, f: (0, f)),
            pl.BlockSpec((tf, d), lambda i, f: (f, 0)),
            pl.BlockSpec((1, d), lambda i, f: (0, 0)),
        ],
        out_specs=pl.BlockSpec((tm, d), lambda i, f: (i, 0)),
        out_shape=jax.ShapeDtypeStruct((m, d), F32),
        compiler_params=pltpu.CompilerParams(
            dimension_semantics=("parallel", "arbitrary"), vmem_limit_bytes=VMEM_LIMIT_BYTES),
        name="mlp",
    )(x1, h, ss, w_up, w_down, final_g)


def _trunk(x, w_in, q_g, k_g, conv_w, attn_g, conv_g, mix_g, mlp_g, final_g, cos, sin, late_weights, cast_late):
    n_batch, seq_len, d = x.shape
    attn_width = attn_g.shape[-1]
    conv_width = conv_g.shape[-1]
    kv_width = (w_in.shape[1] - attn_width - 3 * conv_width) // 2
    x2 = x.reshape(n_batch * seq_len, d)
    w_out, w_up, w_down = late_weights
    z, *cast = _in_proj(x2, mix_g, w_in, (w_out, w_down) if cast_late else ())
    if cast_late:
        w_out, w_down = cast
    a, c, *cast = _attention(z, q_g, k_g, cos, sin, attn_g, conv_w, conv_g, (w_up,) if cast_late else (),
                             n_batch=n_batch, seq_len=seq_len, attn_width=attn_width, kv_width=kv_width)
    if cast_late:
        (w_up,) = cast
    x1, h, ss = _out_proj(x2, a, c, w_out, mlp_g)
    y = _mlp(x1, h, ss, w_up, w_down, final_g)
    return y.reshape(n_batch, seq_len, d), (w_out, w_up, w_down)


def kernel(x_prompt, x_sample, w_in, q_norm, k_norm, conv_w, attn_grp_norm, conv_grp_norm, w_out,
           mix_norm, mlp_norm, w_up, w_down, final_norm):
    assert w_in.shape[0] == 1, "single-layer trunk"
    head_dim = q_norm.shape[-1]
    params = (w_in[0].astype(BF16), q_norm, k_norm, conv_w[0], attn_grp_norm, conv_grp_norm,
              mix_norm, mlp_norm, final_norm.reshape(1, -1))
    late_weights, cast_late = (w_out[0], w_up[0], w_down[0]), True
    outs = []
    for x in (x_prompt, x_sample):
        cos, sin = _rope_tables(x.shape[1], head_dim)
        y, late_weights = _trunk(x, *params, cos, sin, late_weights, cast_late)
        cast_late = False
        outs.append(y)
    return tuple(outs)
```
